```python
import jax, jax.numpy as jnp
from jax import lax
import numpy as np

D_MODEL = 1024
BATCH = 4
SEQ = 4096
DEPTH = 1
DEC_BATCH = 8
DEC_SEQ = 16
PAST_LEN = 2048

CHUNK = 64
MLP_CHUNK = 128
D_MIX = D_MODEL
D_A = D_MIX // 2
A_GROUPS = 4
A_GROUP_CH = D_A // A_GROUPS
D_B = D_MIX - D_A
N_B_HEADS = 8
HEAD_DIM = D_B // N_B_HEADS
Q_BLOCK = 128
N_EXPERTS = 32
TOP_K = 4
D_FF = D_MODEL
SWIGLU_ALPHA = 1.702
SWIGLU_LIMIT = 7.0
RMS_EPS = 1e-6
FORGET_BIAS_MEAN = 3.0
D_IN = 2 * D_A + 3 * D_B + N_B_HEADS
IN_SPLITS = (D_A, 2 * D_A, 2 * D_A + D_B, 2 * D_A + 2 * D_B, 2 * D_A + 3 * D_B)

kernel_name = "hybrid_gmlp_fox_moe_stream_step"


def rmsnorm(x, g):
    xf = x.astype(jnp.float32)
    y = xf * lax.rsqrt(jnp.mean(xf * xf, axis=-1, keepdims=True) + RMS_EPS)
    return (y * g.astype(jnp.float32)).astype(x.dtype)


def in_projection(x, norm_mix, w_in, b_f):
    B, T, _ = x.shape
    z = rmsnorm(x, norm_mix) @ w_in
    u, va, q, k, v, fl = jnp.split(z, IN_SPLITS, axis=-1)
    heads = lambda t: t.reshape(B, T, N_B_HEADS, HEAD_DIM)
    logf = jax.nn.log_sigmoid((fl + b_f).astype(jnp.float32))
    return u, va, heads(q), heads(k), heads(v), logf


def gmlp_gates(u, va, norm_av):
    u = jax.nn.gelu(u, approximate=False)
    vn = rmsnorm(jax.nn.gelu(va, approximate=False), norm_av)
    return u, vn


def chunk_mask(n):
    i = jnp.arange(n)
    return (i[None, :] // CHUNK) <= (i[:, None] // CHUNK)


def gmlp_prompt(u, vn, w_s, b_s):
    B, S, _ = vn.shape
    nc = S // MLP_CHUNK
    vg = vn.reshape(B, nc, MLP_CHUNK, A_GROUPS, A_GROUP_CH)
    w = jnp.where(chunk_mask(MLP_CHUNK), w_s, 0.0)
    mixed = jnp.einsum('gij,bnjgc->bnigc', w, vg) + b_s.T[:, :, None]
    return u * mixed.reshape(B, S, D_A)


def gmlp_sample(u, vn, w_s, b_s):
    B, T, _ = vn.shape
    vg = vn.reshape(B, T, A_GROUPS, A_GROUP_CH)
    w = jnp.where(chunk_mask(MLP_CHUNK)[:T, :T], w_s[:, :T, :T], 0.0)
    mixed = jnp.einsum('gij,bjgc->bigc', w, vg) + b_s[:, :T].T[:, :, None]
    return u * mixed.reshape(B, T, D_A)


def fox_attend(q, k, v, c_q, c_k, q_pos, k_pos):
    s = jnp.einsum('bqhd,bkhd->bhqk', q, k).astype(jnp.float32) * (HEAD_DIM ** -0.5)
    s = s + (c_q.transpose(0, 2, 1)[..., :, None] - c_k.transpose(0, 2, 1)[..., None, :])
    s = jnp.where(k_pos[None, :] <= q_pos[:, None], s, -jnp.inf)
    p = jax.nn.softmax(s, axis=-1).astype(v.dtype)
    return jnp.einsum('bhqk,bkhd->bqhd', p, v)


def fox_prompt(q, k, v, logf):
    B, S, H, dh = q.shape
    c = jnp.cumsum(logf, axis=1)
    nb = S // Q_BLOCK
    pos = jnp.arange(S)
    qb = q.reshape(B, nb, Q_BLOCK, H, dh).swapaxes(0, 1)
    cb = c.reshape(B, nb, Q_BLOCK, H).swapaxes(0, 1)
    pb = pos.reshape(nb, Q_BLOCK)
    o = lax.map(lambda a: fox_attend(a[0], k, v, a[1], c, a[2], pos), (qb, cb, pb))
    return o.swapaxes(0, 1).reshape(B, S, H * dh)


def fox_sample(q, k_new, v_new, logf_new, cache_k, cache_v, cache_logf):
    B, T, H, dh = q.shape
    P = cache_k.shape[1]
    k = jnp.concatenate([cache_k, k_new], axis=1)
    v = jnp.concatenate([cache_v, v_new], axis=1)
    c = jnp.cumsum(jnp.concatenate([cache_logf.astype(jnp.float32), logf_new], axis=1), axis=1)
    pos = jnp.arange(P + T)
    o = fox_attend(q, k, v, c[:, P:], c, pos[P:], pos)
    return o.reshape(B, T, H * dh)


def moe(h, w_router, b_router, w1, b1, w2, b2):
    B, T, D = h.shape
    x = h.reshape(B * T, D)
    logits = (x @ w_router + b_router).astype(jnp.float32)
    top_val, top_idx = lax.top_k(logits, TOP_K)
    top_w = jax.nn.softmax(top_val, axis=-1)
    gates = jnp.einsum('nk,nke->ne', top_w, jax.nn.one_hot(top_idx, N_EXPERTS, dtype=jnp.float32))
    y = jnp.zeros((B * T, D), jnp.float32)
    for e in range(N_EXPERTS):
        hid = x @ w1[e] + b1[e]
        glu = jnp.minimum(hid[:, :D_FF], SWIGLU_LIMIT)
        lin = jnp.clip(hid[:, D_FF:], -SWIGLU_LIMIT, SWIGLU_LIMIT)
        act = glu * jax.nn.sigmoid(SWIGLU_ALPHA * glu) * (lin + 1.0)
        y = y + gates[:, e:e + 1] * (act @ w2[e] + b2[e]).astype(jnp.float32)
    return y.reshape(B, T, D).astype(h.dtype)


def layer_tail(x, out_a, out_b, norm_out_a, norm_out_b, w_out, norm_ffn,
               w_router, b_router, w1, b1, w2, b2):
    mixed = jnp.concatenate([rmsnorm(out_a, norm_out_a), rmsnorm(out_b, norm_out_b)], axis=-1)
    x = x + mixed @ w_out
    return x + moe(rmsnorm(x, norm_ffn), w_router, b_router, w1, b1, w2, b2)


def setup_inputs(seed: int = 0) -> dict:
    key = jax.random.key(seed)
    ks = jax.random.split(key, 24)
    nrm = lambda k, shape, s: s * jax.random.normal(k, shape, jnp.float32)
    L = DEPTH
    return {
        "x_prompt": nrm(ks[0], (BATCH, SEQ, D_MODEL), 1.0),
        "x_sample": nrm(ks[1], (DEC_BATCH, DEC_SEQ, D_MODEL), 1.0),
        "cache_k": nrm(ks[2], (L, DEC_BATCH, PAST_LEN, N_B_HEADS, HEAD_DIM), 1.0),
        "cache_v": nrm(ks[3], (L, DEC_BATCH, PAST_LEN, N_B_HEADS, HEAD_DIM), 1.0),
        "cache_logf": jax.nn.log_sigmoid(FORGET_BIAS_MEAN + nrm(ks[4], (L, DEC_BATCH, PAST_LEN, N_B_HEADS), 1.0)),
        "norm_mix": 1.0 + nrm(ks[5], (L, D_MODEL), 0.05),
        "w_in": nrm(ks[6], (L, D_MODEL, D_IN), D_MODEL ** -0.5),
        "b_f": FORGET_BIAS_MEAN + nrm(ks[7], (L, N_B_HEADS), 0.5),
        "norm_av": 1.0 + nrm(ks[8], (L, D_A), 0.05),
        "w_s": nrm(ks[9], (L, A_GROUPS, MLP_CHUNK, MLP_CHUNK), MLP_CHUNK ** -0.5),
        "b_s": 1.0 + nrm(ks[10], (L, A_GROUPS, MLP_CHUNK), 0.1),
        "norm_out_a": 1.0 + nrm(ks[11], (L, D_A), 0.05),
        "norm_out_b": 1.0 + nrm(ks[12], (L, D_B), 0.05),
        "w_out": nrm(ks[13], (L, D_MIX, D_MODEL), D_MIX ** -0.5),
        "norm_ffn": 1.0 + nrm(ks[14], (L, D_MODEL), 0.05),
        "w_router": nrm(ks[15], (L, D_MODEL, N_EXPERTS), D_MODEL ** -0.5),
        "b_router": nrm(ks[16], (L, N_EXPERTS), 0.01),
        "w1": nrm(ks[17], (L, N_EXPERTS, D_MODEL, 2 * D_FF), D_MODEL ** -0.5),
        "b1": nrm(ks[18], (L, N_EXPERTS, 2 * D_FF), 0.01),
        "w2": nrm(ks[19], (L, N_EXPERTS, D_FF, D_MODEL), D_FF ** -0.5),
        "b2": nrm(ks[20], (L, N_EXPERTS, D_MODEL), 0.01),
        "norm_final": 1.0 + nrm(ks[21], (D_MODEL,), 0.05),
    }


def reference(x_prompt, x_sample, cache_k, cache_v, cache_logf, norm_mix, w_in, b_f, norm_av,
              w_s, b_s, norm_out_a, norm_out_b, w_out, norm_ffn, w_router, b_router,
              w1, b1, w2, b2, norm_final):
    xp, xs = x_prompt, x_sample
    kp, vp, lfp, ksm, vsm, lfs, gvs = [], [], [], [], [], [], []
    for l in range(DEPTH):
        tail = lambda x, a, b: layer_tail(x, a, b, norm_out_a[l], norm_out_b[l], w_out[l], norm_ffn[l],
                                          w_router[l], b_router[l], w1[l], b1[l], w2[l], b2[l])
        u, va, q, k, v, lf = in_projection(xp, norm_mix[l], w_in[l], b_f[l])
        u, vn = gmlp_gates(u, va, norm_av[l])
        xp = tail(xp, gmlp_prompt(u, vn, w_s[l], b_s[l]), fox_prompt(q, k, v, lf))
        kp.append(k); vp.append(v); lfp.append(lf)
        u, va, q, k, v, lf = in_projection(xs, norm_mix[l], w_in[l], b_f[l])
        u, vn = gmlp_gates(u, va, norm_av[l])
        xs = tail(xs, gmlp_sample(u, vn, w_s[l], b_s[l]),
                  fox_sample(q, k, v, lf, cache_k[l], cache_v[l], cache_logf[l]))
        ksm.append(k); vsm.append(v); lfs.append(lf); gvs.append(vn)
    y_prompt = rmsnorm(xp, norm_final)
    y_sample = rmsnorm(xs, norm_final)
    k_prompt, v_prompt, logf_prompt = jnp.stack(kp), jnp.stack(vp), jnp.stack(lfp)
    k_sample, v_sample, logf_sample = jnp.stack(ksm), jnp.stack(vsm), jnp.stack(lfs)
    gmlp_v_sample = jnp.stack(gvs)
    return (y_prompt, y_sample, k_prompt, v_prompt, logf_prompt, k_sample, v_sample, logf_sample, gmlp_v_sample)
```

```python
import functools

import jax
import jax.numpy as jnp
import numpy as np
from jax import lax
from jax.experimental import pallas as pl
from jax.experimental.pallas import tpu as pltpu

F32 = jnp.float32
BF16 = jnp.bfloat16
I32 = jnp.int32

D_MODEL = 1024
CHUNK = 64
MLP_CHUNK = 128
D_A = 512
A_GROUPS = 4
A_GROUP_CH = D_A // A_GROUPS
D_B = 512
N_B_HEADS = 8
HEAD_DIM = D_B // N_B_HEADS
N_EXPERTS = 32
TOP_K = 4
D_FF = D_MODEL
SWIGLU_ALPHA = 1.702
SWIGLU_LIMIT = 7.0
RMS_EPS = 1e-6

LANES = 128
SUBLANES = 8
ROW_TILES = D_MODEL // LANES
HEADS_PER_BLOCK = LANES // HEAD_DIM
N_HEAD_BLOCKS = N_B_HEADS // HEADS_PER_BLOCK
QKV_OFF = 2 * D_A
F_OFF = 2 * D_A + 3 * D_B
W_IN_COLS = F_OFF + LANES

TM_PROJ = 512
TM_TOK = 256
TM_MOE = 256
TQ = 256
CS_BLK = 256
FF_BLK = 512
VMEM_LIMIT = 48 * 1024 * 1024


def _cparams(sem, vmem=VMEM_LIMIT):
    return pltpu.CompilerParams(dimension_semantics=sem, vmem_limit_bytes=vmem)


def _gelu(x):
    return 0.5 * x * (1.0 + lax.erf(x * 0.7071067811865476))


def _log_sigmoid(x):
    return jnp.minimum(x, 0.0) - jnp.log1p(jnp.exp(-jnp.abs(x)))


def _sigmoid(x):
    return 1.0 / (1.0 + jnp.exp(-x))


def _rms(x):
    return x * lax.rsqrt(jnp.mean(x * x, axis=-1, keepdims=True) + RMS_EPS)


def _div_pow2(x, d):
    assert d & (d - 1) == 0
    return jnp.right_shift(x, d.bit_length() - 1)


def _mod_pow2(x, d):
    assert d & (d - 1) == 0
    return jnp.bitwise_and(x, d - 1)


def _inproj_kernel(x_ref, nm_ref, win_ref, bf_ref, nav_ref, ws_ref, bst_ref,
                   outa_ref, q_ref, k_ref, v_ref, kb_ref, vb_ref, logf_ref, vn_ref, *, period):
    tm = x_ref.shape[0]
    hn = (_rms(x_ref[...]) * nm_ref[...]).astype(BF16)

    def proj(lo, width):
        return jnp.dot(hn, win_ref[:, lo:lo + width], preferred_element_type=F32)

    u = _gelu(proj(0, D_A))
    vn = _rms(_gelu(proj(D_A, D_A))) * nav_ref[...]
    vn_ref[...] = vn
    q_ref[...] = (proj(QKV_OFF, D_B) * (HEAD_DIM ** -0.5)).astype(BF16)
    kk = proj(QKV_OFF + D_B, D_B)
    k_ref[...] = kk
    kb_ref[...] = kk.astype(BF16)
    vv = proj(QKV_OFF + 2 * D_B, D_B)
    v_ref[...] = vv
    vb_ref[...] = vv.astype(BF16)
    fl = proj(F_OFF, LANES) + bf_ref[...]
    logf_ref[...] = _log_sigmoid(fl)[:, :N_B_HEADS]

    vnb = vn.astype(BF16)
    ri = lax.broadcasted_iota(I32, (MLP_CHUNK, MLP_CHUNK), 0)
    ci = lax.broadcasted_iota(I32, (MLP_CHUNK, MLP_CHUNK), 1)
    same_stream = _div_pow2(ri, period) == _div_pow2(ci, period)
    visible = _div_pow2(_mod_pow2(ci, period), CHUNK) <= _div_pow2(_mod_pow2(ri, period), CHUNK)
    mask = jnp.logical_and(same_stream, visible)
    for g in range(A_GROUPS):
        w = jnp.where(mask, ws_ref[g], 0.0).astype(BF16)
        bcol = bst_ref[:, g:g + 1]
        cols = slice(g * A_GROUP_CH, (g + 1) * A_GROUP_CH)
        for c in range(tm // MLP_CHUNK):
            rows = slice(c * MLP_CHUNK, (c + 1) * MLP_CHUNK)
            mixed = jnp.dot(w, vnb[rows, cols], preferred_element_type=F32) + bcol
            outa_ref[rows, cols] = u[rows, cols] * mixed


def _in_projection(x2d, norm_mix, win_b, bf_pad, norm_av, ws_tiled, bst_tiled, *, tm, period):
    rows = x2d.shape[0]
    row_blk = lambda w: pl.BlockSpec((tm, w), lambda i: (i, 0))
    full = lambda shape: pl.BlockSpec(shape, lambda i: (0,) * len(shape))
    out_shape = (
        jax.ShapeDtypeStruct((rows, D_A), F32),
        jax.ShapeDtypeStruct((rows, D_B), BF16),
        jax.ShapeDtypeStruct((rows, D_B), F32),
        jax.ShapeDtypeStruct((rows, D_B), F32),
        jax.ShapeDtypeStruct((rows, D_B), BF16),
        jax.ShapeDtypeStruct((rows, D_B), BF16),
        jax.ShapeDtypeStruct((rows, N_B_HEADS), F32),
        jax.ShapeDtypeStruct((rows, D_A), F32),
    )
    return pl.pallas_call(
        functools.partial(_inproj_kernel, period=period),
        grid=(rows // tm,),
        in_specs=[row_blk(D_MODEL), full((1, D_MODEL)), full((D_MODEL, W_IN_COLS)), full((1, LANES)),
                  full((1, D_A)), full((A_GROUPS, MLP_CHUNK, MLP_CHUNK)), full((MLP_CHUNK, A_GROUPS))],
        out_specs=(row_blk(D_A), row_blk(D_B), row_blk(D_B), row_blk(D_B), row_blk(D_B), row_blk(D_B),
                   row_blk(N_B_HEADS), row_blk(D_A)),
        out_shape=out_shape,
        compiler_params=_cparams(("arbitrary",)),
        name="in_projection",
    )(x2d, norm_mix, win_b, bf_pad, norm_av, ws_tiled, bst_tiled)


def _cumsum_kernel(x_ref, o_ref):
    t = x_ref.shape[2]
    ri = lax.broadcasted_iota(I32, (CS_BLK, CS_BLK), 0)
    ci = lax.broadcasted_iota(I32, (CS_BLK, CS_BLK), 1)
    upper = jnp.where(ri <= ci, 1.0, 0.0).astype(BF16)
    carry = jnp.zeros((N_B_HEADS, 1), F32)
    for c in range(t // CS_BLK):
        x = x_ref[0, :, c * CS_BLK:(c + 1) * CS_BLK]
        x1 = x.astype(BF16)
        r1 = x - x1.astype(F32)
        x2 = r1.astype(BF16)
        x3 = (r1 - x2.astype(F32)).astype(BF16)
        s = (jnp.dot(x1, upper, preferred_element_type=F32)
             + jnp.dot(x2, upper, preferred_element_type=F32)
             + jnp.dot(x3, upper, preferred_element_type=F32)) + carry
        o_ref[0, :, c * CS_BLK:(c + 1) * CS_BLK] = s
        carry = s[:, CS_BLK - 1:CS_BLK]


def _cumsum_time(x_bht):
    b, h, t = x_bht.shape
    return pl.pallas_call(
        _cumsum_kernel,
        grid=(b,),
        in_specs=[pl.BlockSpec((1, h, t), lambda i: (i, 0, 0))],
        out_specs=pl.BlockSpec((1, h, t), lambda i: (i, 0, 0)),
        out_shape=jax.ShapeDtypeStruct((b, h, t), F32),
        compiler_params=_cparams(("arbitrary",)),
        name="cumsum_logf",
    )(x_bht)


def _head_select(width=LANES):
    lane = lax.broadcasted_iota(I32, (1, width), 1)
    return [jnp.logical_and(lane >= h * HEAD_DIM, lane < (h + 1) * HEAD_DIM) for h in range(HEADS_PER_BLOCK)]


def _scores(qh, k2):
    return lax.dot_general(qh, k2, (((1,), (1,)), ((), ())), preferred_element_type=F32)


def _attn_prompt_kernel(q_ref, k_ref, v_ref, c_ref, o_ref, m_sc, l_sc, acc_sc):
    i = pl.program_id(2)
    sel = _head_select()
    q2 = q_ref[0]
    zero = jnp.zeros_like(q2)
    qh = [jnp.where(sel[h], q2, zero) for h in range(HEADS_PER_BLOCK)]
    q0 = pl.multiple_of(i * TQ, TQ)
    cbase = [c_ref[0, 0, h:h + 1, pl.ds(q0, LANES)][:, 0:1] for h in range(HEADS_PER_BLOCK)]

    m_sc[...] = jnp.full(m_sc.shape, -jnp.inf, F32)
    l_sc[...] = jnp.zeros(l_sc.shape, F32)
    acc_sc[...] = jnp.zeros(acc_sc.shape, F32)

    def step(j, masked):
        k0 = pl.multiple_of(j * TQ, TQ)
        k2 = k_ref[0, pl.ds(k0, TQ), :]
        v2 = v_ref[0, pl.ds(k0, TQ), :]
        pv, alpha = [], []
        for h in range(HEADS_PER_BLOCK):
            s = _scores(qh[h], k2) + (cbase[h] - c_ref[0, 0, h:h + 1, pl.ds(k0, TQ)])
            if masked:
                row = lax.broadcasted_iota(I32, (TQ, TQ), 0)
                col = lax.broadcasted_iota(I32, (TQ, TQ), 1)
                s = jnp.where(col <= row, s, -jnp.inf)
            m_prev = m_sc[h]
            m_new = jnp.maximum(m_prev, jnp.max(s, axis=-1, keepdims=True))
            a = jnp.exp(m_prev - m_new)
            p = jnp.exp(s - m_new)
            l_sc[h] = a * l_sc[h] + jnp.sum(p, axis=-1, keepdims=True)
            m_sc[h] = m_new
            alpha.append(a)
            pv.append(jnp.dot(p.astype(BF16), v2, preferred_element_type=F32))
        acc = acc_sc[...]
        acc_sc[...] = jnp.where(sel[0], acc * alpha[0] + pv[0], acc * alpha[1] + pv[1])

    def body(j, carry):
        step(j, False)
        return carry

    lax.fori_loop(0, i, body, 0)
    step(i, True)
    o_ref[0] = acc_sc[...] / jnp.where(sel[0], l_sc[0], l_sc[1])


def _attention_prompt(q, kb, vb, c_pairs):
    b, s, _ = q.shape
    nq = s // TQ
    qblk = pl.BlockSpec((1, TQ, LANES), lambda bi, hp, i: (bi, i, hp))
    kvblk = pl.BlockSpec((1, s, LANES), lambda bi, hp, i: (bi, 0, hp))
    cblk = pl.BlockSpec((1, 1, HEADS_PER_BLOCK, s), lambda bi, hp, i: (bi, hp, 0, 0))
    return pl.pallas_call(
        _attn_prompt_kernel,
        grid=(b, N_HEAD_BLOCKS, nq),
        in_specs=[qblk, kvblk, kvblk, cblk],
        out_specs=pl.BlockSpec((1, TQ, LANES), lambda bi, hp, i: (bi, i, hp)),
        out_shape=jax.ShapeDtypeStruct((b, s, D_B), F32),
        scratch_shapes=[pltpu.VMEM((HEADS_PER_BLOCK, TQ, 1), F32),
                        pltpu.VMEM((HEADS_PER_BLOCK, TQ, 1), F32),
                        pltpu.VMEM((TQ, LANES), F32)],
        compiler_params=_cparams(("arbitrary", "arbitrary", "arbitrary")),
        name="fox_attention_prompt",
    )(q, kb, vb, c_pairs)


def _attn_sample_kernel(q_ref, kc_ref, vc_ref, kn_ref, vn_ref, cc_ref, cn_ref, o_ref):
    t = q_ref.shape[1]
    sel = _head_select()
    q2 = q_ref[0]
    zero = jnp.zeros_like(q2)
    kc = kc_ref[0].astype(BF16)
    vc = vc_ref[0].astype(BF16)
    kn = kn_ref[0]
    vn = vn_ref[0]
    row = lax.broadcasted_iota(I32, (t, t), 0)
    col = lax.broadcasted_iota(I32, (t, t), 1)
    outs = []
    for h in range(HEADS_PER_BLOCK):
        qh = jnp.where(sel[h], q2, zero)
        cbase = cn_ref[0, 0, h:h + 1, 0:1]
        s_c = _scores(qh, kc) + (cbase - cc_ref[0, 0, h:h + 1, :])
        s_n = _scores(qh, kn) + (cbase - cn_ref[0, 0, h:h + 1, :])
        s_n = jnp.where(col <= row, s_n, -jnp.inf)
        m = jnp.maximum(jnp.max(s_c, axis=-1, keepdims=True), jnp.max(s_n, axis=-1, keepdims=True))
        p_c = jnp.exp(s_c - m)
        p_n = jnp.exp(s_n - m)
        l = jnp.sum(p_c, axis=-1, keepdims=True) + jnp.sum(p_n, axis=-1, keepdims=True)
        o = (jnp.dot(p_c.astype(BF16), vc, preferred_element_type=F32)
             + jnp.dot(p_n.astype(BF16), vn, preferred_element_type=F32))
        outs.append(o / l)
    o_ref[0] = jnp.where(sel[0], outs[0], outs[1])


def _attention_sample(q, cache_k, cache_v, kb, vb, c_cache, c_new):
    b, t, _ = q.shape
    p = cache_k.shape[1]
    newblk = pl.BlockSpec((1, t, LANES), lambda bi, hp: (bi, 0, hp))
    cacheblk = pl.BlockSpec((1, p, LANES), lambda bi, hp: (bi, 0, hp))
    return pl.pallas_call(
        _attn_sample_kernel,
        grid=(b, N_HEAD_BLOCKS),
        in_specs=[newblk, cacheblk, cacheblk, newblk, newblk,
                  pl.BlockSpec((1, 1, HEADS_PER_BLOCK, p), lambda bi, hp: (bi, hp, 0, 0)),
                  pl.BlockSpec((1, 1, HEADS_PER_BLOCK, t), lambda bi, hp: (bi, hp, 0, 0))],
        out_specs=newblk,
        out_shape=jax.ShapeDtypeStruct((b, t, D_B), F32),
        compiler_params=_cparams(("arbitrary", "arbitrary")),
        name="fox_attention_sample",
    )(q, cache_k, cache_v, kb, vb, c_cache, c_new)


def _tail_kernel(oap_ref, obp_ref, xp_ref, oas_ref, obs_ref, xs_ref, noa_ref, nob_ref, wout_ref,
                 nffn_ref, wr_ref, br_ref,
                 x2_ref, h3_ref, tidx_ref, tw_ref, rank_ref, cnt_ref, carry_sc):
    i = pl.program_id(0)
    is_sample = i == pl.num_programs(0) - 1
    oa = jnp.where(is_sample, oas_ref[...], oap_ref[...])
    ob = jnp.where(is_sample, obs_ref[...], obp_ref[...])
    x = jnp.where(is_sample, xs_ref[...], xp_ref[...])

    ma = (_rms(oa) * noa_ref[...]).astype(BF16)
    mb = (_rms(ob) * nob_ref[...]).astype(BF16)
    x2 = x + (jnp.dot(ma, wout_ref[:D_A, :], preferred_element_type=F32)
              + jnp.dot(mb, wout_ref[D_A:, :], preferred_element_type=F32))
    x2_ref[...] = x2
    h = _rms(x2) * nffn_ref[...]
    for j in range(ROW_TILES):
        h3_ref[:, j, :] = h[:, j * LANES:(j + 1) * LANES]

    lane = lax.broadcasted_iota(I32, (TM_TOK, LANES), 1)
    logits = jnp.dot(h.astype(BF16), wr_ref[...], preferred_element_type=F32) + br_ref[...]
    logits = jnp.where(lane < N_EXPERTS, logits, -jnp.inf)

    lane_f = lane.astype(F32)
    vals, idxs, hots = [], [], []
    work = logits
    for _ in range(TOP_K):
        mx = jnp.max(work, axis=-1, keepdims=True)
        idx = jnp.min(jnp.where(work == mx, lane_f, float(LANES)), axis=-1, keepdims=True)
        hot = lane_f == idx
        vals.append(mx)
        idxs.append(idx.astype(I32))
        hots.append(hot)
        work = jnp.where(hot, -jnp.inf, work)
    exps = [jnp.exp(v - vals[0]) for v in vals]
    denom = exps[0] + exps[1] + exps[2] + exps[3]

    @pl.when(i == 0)
    def _():
        carry_sc[...] = jnp.zeros(carry_sc.shape, F32)

    assigned = jnp.zeros((TM_TOK, LANES), F32)
    for hot in hots:
        assigned = assigned + jnp.where(hot, 1.0, 0.0)
    ri = lax.broadcasted_iota(I32, (TM_TOK, TM_TOK), 0)
    ci = lax.broadcasted_iota(I32, (TM_TOK, TM_TOK), 1)
    earlier = jnp.where(ci < ri, 1.0, 0.0).astype(BF16)
    before = jnp.dot(earlier, assigned.astype(BF16), preferred_element_type=F32) + carry_sc[...]

    tidx = jnp.zeros((TM_TOK, LANES), I32)
    tw = jnp.zeros((TM_TOK, LANES), F32)
    rank = jnp.zeros((TM_TOK, LANES), I32)
    for k in range(TOP_K):
        rk = jnp.sum(jnp.where(hots[k], before, 0.0), axis=-1, keepdims=True).astype(I32)
        tidx = jnp.where(lane == k, idxs[k], tidx)
        tw = jnp.where(lane == k, exps[k] / denom, tw)
        rank = jnp.where(lane == k, rk, rank)
    tidx_ref[...] = tidx
    tw_ref[...] = tw
    rank_ref[...] = rank
    total = carry_sc[...] + jnp.sum(assigned, axis=0, keepdims=True)
    carry_sc[...] = total
    cnt_ref[...] = jnp.broadcast_to(total, cnt_ref.shape)


def _tail(oa_p, ob_p, x_p, oa_s, ob_s, x_s, noa, nob, wout_b, nffn, wr_b, br_pad):
    n_prompt_tiles = x_p.shape[0] // TM_TOK
    n_tiles = n_prompt_tiles + 1
    n_pad = n_tiles * TM_TOK
    pblk = lambda w: pl.BlockSpec((TM_TOK, w), lambda i: (jnp.minimum(i, n_prompt_tiles - 1), 0))
    full = lambda shape: pl.BlockSpec(shape, lambda i: (0,) * len(shape))
    oblk = lambda w: pl.BlockSpec((TM_TOK, w), lambda i: (i, 0))
    out_shape = (
        jax.ShapeDtypeStruct((n_pad, D_MODEL), F32),
        jax.ShapeDtypeStruct((n_pad, ROW_TILES, LANES), F32),
        jax.ShapeDtypeStruct((n_pad, LANES), I32),
        jax.ShapeDtypeStruct((n_pad, LANES), F32),
        jax.ShapeDtypeStruct((n_pad, LANES), I32),
        jax.ShapeDtypeStruct((SUBLANES, LANES), F32),
    )
    return pl.pallas_call(
        _tail_kernel,
        grid=(n_tiles,),
        in_specs=[pblk(D_A), pblk(D_B), pblk(D_MODEL), full((TM_TOK, D_A)), full((TM_TOK, D_B)),
                  full((TM_TOK, D_MODEL)), full((1, D_A)), full((1, D_B)), full((D_MODEL, D_MODEL)),
                  full((1, D_MODEL)), full((D_MODEL, LANES)), full((1, LANES))],
        out_specs=(oblk(D_MODEL), pl.BlockSpec((TM_TOK, ROW_TILES, LANES), lambda i: (i, 0, 0)),
                   oblk(LANES), oblk(LANES), oblk(LANES), full((SUBLANES, LANES))),
        out_shape=out_shape,
        scratch_shapes=[pltpu.VMEM((1, LANES), F32)],
        compiler_params=_cparams(("arbitrary",)),
        name="outproj_router",
    )(oa_p, ob_p, x_p, oa_s, ob_s, x_s, noa, nob, wout_b, nffn, wr_b, br_pad)


def _row_copy(src, dst, sem):
    return pltpu.make_async_copy(src, dst, sem)


def _dispatch_kernel(pad_start_ref, pad_len_ref, nu_ref, slot_hbm, h3_ref, xs_hbm, slot_sm, zblk, sem, ssem):
    i = pl.program_id(0)
    n_assign = TM_TOK * TOP_K
    cp = pltpu.make_async_copy(slot_hbm.at[pl.ds(i * n_assign, n_assign)], slot_sm, ssem)
    cp.start()
    cp.wait()

    def issue(t, carry):
        for k in range(TOP_K):
            _row_copy(h3_ref.at[t], xs_hbm.at[slot_sm[t * TOP_K + k]], sem).start()
        return carry

    def drain(t, carry):
        for k in range(TOP_K):
            _row_copy(h3_ref.at[t], xs_hbm.at[slot_sm[t * TOP_K + k]], sem).wait()
        return carry

    lax.fori_loop(0, TM_TOK, issue, 0)
    lax.fori_loop(0, TM_TOK, drain, 0)

    @pl.when(i == pl.num_programs(0) - 1)
    def _():
        zblk[...] = jnp.zeros(zblk.shape, F32)
        for e in range(N_EXPERTS):
            start = pad_start_ref[e]
            n = pad_len_ref[e]

            def zissue(r, carry):
                _row_copy(zblk.at[0], xs_hbm.at[start + r], sem).start()
                return carry

            def zdrain(r, carry):
                _row_copy(zblk.at[0], xs_hbm.at[start + r], sem).wait()
                return carry

            lax.fori_loop(0, n, zissue, 0)
            lax.fori_loop(0, n, zdrain, 0)

        def tile_copy(r):
            return pltpu.make_async_copy(zblk, xs_hbm.at[pl.ds(r * TM_MOE, TM_MOE)], sem)

        def tissue(r, carry):
            tile_copy(r).start()
            return carry

        def tdrain(r, carry):
            tile_copy(r).wait()
            return carry

        n_tiles_moe = xs_hbm.shape[0] // TM_MOE
        lax.fori_loop(nu_ref[0], n_tiles_moe, tissue, 0)
        lax.fori_loop(nu_ref[0], n_tiles_moe, tdrain, 0)


def _dispatch(pad_start, pad_len, n_used, slot_flat, h3, n_slots):
    n_tiles = h3.shape[0] // TM_TOK
    grid_spec = pltpu.PrefetchScalarGridSpec(
        num_scalar_prefetch=3,
        grid=(n_tiles,),
        in_specs=[pl.BlockSpec(memory_space=pl.ANY),
                  pl.BlockSpec((TM_TOK, ROW_TILES, LANES), lambda i, ps, pn, nu: (i, 0, 0))],
        out_specs=pl.BlockSpec(memory_space=pl.ANY),
        scratch_shapes=[pltpu.SMEM((TM_TOK * TOP_K,), I32), pltpu.VMEM((TM_MOE, ROW_TILES, LANES), F32),
                        pltpu.SemaphoreType.DMA, pltpu.SemaphoreType.DMA],
    )
    return pl.pallas_call(
        _dispatch_kernel,
        grid_spec=grid_spec,
        out_shape=jax.ShapeDtypeStruct((n_slots, ROW_TILES, LANES), F32),
        compiler_params=_cparams(("arbitrary",)),
        name="moe_dispatch",
    )(pad_start, pad_len, n_used, slot_flat, h3)


def _moe_kernel(te_ref, nu_ref, xs_ref, w1_ref, b1_ref, w2_ref, b2_ref, ys_ref, w1b_sc, w2b_sc):
    i = pl.program_id(0)
    prev = te_ref[jnp.maximum(i - 1, 0)]
    new_expert = jnp.logical_or(i == 0, te_ref[i] != prev)

    @pl.when(new_expert)
    def _():
        blk = 128
        for r in range(D_MODEL // blk):
            w1b_sc[r * blk:(r + 1) * blk, :] = w1_ref[0, r * blk:(r + 1) * blk, :].astype(BF16)
        for r in range(D_FF // blk):
            w2b_sc[r * blk:(r + 1) * blk, :] = w2_ref[0, r * blk:(r + 1) * blk, :].astype(BF16)

    @pl.when(i < nu_ref[0])
    def _():
        xb = jnp.concatenate([xs_ref[:, j, :] for j in range(ROW_TILES)], axis=-1).astype(BF16)
        acc = jnp.zeros((TM_MOE, D_MODEL), F32)
        for c in range(D_FF // FF_BLK):
            lo = c * FF_BLK
            glu = jnp.dot(xb, w1b_sc[:, lo:lo + FF_BLK], preferred_element_type=F32) + b1_ref[0, :, lo:lo + FF_BLK]
            lin = (jnp.dot(xb, w1b_sc[:, D_FF + lo:D_FF + lo + FF_BLK], preferred_element_type=F32)
                   + b1_ref[0, :, D_FF + lo:D_FF + lo + FF_BLK])
            glu = jnp.minimum(glu, SWIGLU_LIMIT)
            lin = jnp.clip(lin, -SWIGLU_LIMIT, SWIGLU_LIMIT)
            act = glu * _sigmoid(SWIGLU_ALPHA * glu) * (lin + 1.0)
            acc = acc + jnp.dot(act.astype(BF16), w2b_sc[lo:lo + FF_BLK, :], preferred_element_type=F32)
        out = acc + b2_ref[0]
        for j in range(ROW_TILES):
            ys_ref[:, j, :] = out[:, j * LANES:(j + 1) * LANES]

    @pl.when(i >= nu_ref[0])
    def _():
        ys_ref[...] = jnp.zeros(ys_ref.shape, F32)


def _moe(tile_expert, n_used, xs, w1, b1, w2, b2):
    n_slots = xs.shape[0]
    n_tiles = n_slots // TM_MOE
    slot_map = lambda i, te, nu: (jnp.minimum(i, nu[0] - 1), 0, 0)
    exp_map = lambda i, te, nu: (te[i], 0, 0)
    grid_spec = pltpu.PrefetchScalarGridSpec(
        num_scalar_prefetch=2,
        grid=(n_tiles,),
        in_specs=[pl.BlockSpec((TM_MOE, ROW_TILES, LANES), slot_map),
                  pl.BlockSpec((1, D_MODEL, 2 * D_FF), exp_map),
                  pl.BlockSpec((1, 1, 2 * D_FF), exp_map),
                  pl.BlockSpec((1, D_FF, D_MODEL), exp_map),
                  pl.BlockSpec((1, 1, D_MODEL), exp_map)],
        out_specs=pl.BlockSpec((TM_MOE, ROW_TILES, LANES), lambda i, te, nu: (i, 0, 0)),
        scratch_shapes=[pltpu.VMEM((D_MODEL, 2 * D_FF), BF16), pltpu.VMEM((D_FF, D_MODEL), BF16)],
    )
    return pl.pallas_call(
        _moe_kernel,
        grid_spec=grid_spec,
        out_shape=jax.ShapeDtypeStruct((n_slots, ROW_TILES, LANES), F32),
        compiler_params=_cparams(("arbitrary",), vmem=56 * 1024 * 1024),
        name="moe_experts",
    )(tile_expert, n_used, xs, w1, b1, w2, b2)


def _combine_kernel(slot_hbm, ys_hbm, x2_ref, tw_ref, nf_ref, y_ref, slot_sm, buf, sem, ssem):
    i = pl.program_id(0)
    n_assign = TM_TOK * TOP_K
    cp = pltpu.make_async_copy(slot_hbm.at[pl.ds(i * n_assign, n_assign)], slot_sm, ssem)
    cp.start()
    cp.wait()

    def issue(t, carry):
        for k in range(TOP_K):
            _row_copy(ys_hbm.at[slot_sm[t * TOP_K + k]], buf.at[k * TM_TOK + t], sem).start()
        return carry

    def drain(t, carry):
        for k in range(TOP_K):
            _row_copy(ys_hbm.at[slot_sm[t * TOP_K + k]], buf.at[k * TM_TOK + t], sem).wait()
        return carry

    lax.fori_loop(0, TM_TOK, issue, 0)
    lax.fori_loop(0, TM_TOK, drain, 0)

    tw = tw_ref[...]
    parts = []
    for j in range(ROW_TILES):
        acc = jnp.zeros((TM_TOK, LANES), F32)
        for k in range(TOP_K):
            acc = acc + tw[:, k:k + 1] * buf[k * TM_TOK:(k + 1) * TM_TOK, j, :]
        parts.append(acc)
    x3 = x2_ref[...] + jnp.concatenate(parts, axis=-1)
    y_ref[...] = _rms(x3) * nf_ref[...]


def _combine(slot_flat, ys, x2, tw, norm_final):
    n_tiles = x2.shape[0] // TM_TOK
    return pl.pallas_call(
        _combine_kernel,
        grid=(n_tiles,),
        in_specs=[pl.BlockSpec(memory_space=pl.ANY), pl.BlockSpec(memory_space=pl.ANY),
                  pl.BlockSpec((TM_TOK, D_MODEL), lambda i: (i, 0)),
                  pl.BlockSpec((TM_TOK, LANES), lambda i: (i, 0)),
                  pl.BlockSpec((1, D_MODEL), lambda i: (0, 0))],
        out_specs=pl.BlockSpec((TM_TOK, D_MODEL), lambda i: (i, 0)),
        out_shape=jax.ShapeDtypeStruct(x2.shape, F32),
        scratch_shapes=[pltpu.SMEM((TM_TOK * TOP_K,), I32),
                        pltpu.VMEM((TOP_K * TM_TOK, ROW_TILES, LANES), F32),
                        pltpu.SemaphoreType.DMA, pltpu.SemaphoreType.DMA],
        compiler_params=_cparams(("arbitrary",)),
        name="moe_combine",
    )(slot_flat, ys, x2, tw, norm_final)


def _head_pairs(c_bht):
    b, h, t = c_bht.shape
    return c_bht.reshape(b, N_HEAD_BLOCKS, HEADS_PER_BLOCK, t)


def _routing_tables(tidx, rank, counts, n_tiles_moe):
    padded = ((counts + TM_MOE - 1) // TM_MOE) * TM_MOE
    ends = jnp.cumsum(padded)
    offs = ends - padded
    experts = jnp.arange(N_EXPERTS, dtype=I32)
    off_of = jnp.sum(jnp.where(tidx[..., None] == experts, offs, 0), axis=-1)
    slot = (off_of + rank).reshape(-1).astype(I32)
    n_used = (ends[-1] // TM_MOE).astype(I32)
    tile_start = jnp.arange(n_tiles_moe, dtype=I32) * TM_MOE
    te = jnp.sum((tile_start[:, None] >= ends[None, :]).astype(I32), axis=-1)
    te_last = jnp.sum(((n_used - 1) * TM_MOE >= ends).astype(I32))
    te = jnp.minimum(jnp.where(jnp.arange(n_tiles_moe) < n_used, te, te_last), N_EXPERTS - 1).astype(I32)
    return slot, te, n_used.reshape(1), (offs + counts).astype(I32), (padded - counts).astype(I32)


def kernel(x_prompt, x_sample, cache_k, cache_v, cache_logf, norm_mix, w_in, b_f, norm_av, w_s, b_s,
           norm_out_a, norm_out_b, w_out, norm_ffn, w_router, b_router, w1, b1, w2, b2, norm_final):
    depth = w_in.shape[0]
    bsz, seq, _ = x_prompt.shape
    dbsz, dseq, _ = x_sample.shape
    past = cache_k.shape[2]
    n_prompt = bsz * seq
    n_sample = dbsz * dseq
    assert depth == 1, "the combine kernel applies the final norm, so it must follow the only layer"
    assert n_prompt % TM_PROJ == 0 and n_prompt % TM_TOK == 0 and seq % TQ == 0
    assert n_sample == MLP_CHUNK and MLP_CHUNK % dseq == 0 and n_sample <= TM_TOK
    n_pad = n_prompt + TM_TOK
    n_tiles_moe = (n_pad * TOP_K) // TM_MOE + N_EXPERTS
    n_slots = n_tiles_moe * TM_MOE

    xp = x_prompt.reshape(n_prompt, D_MODEL)
    xs = x_sample.reshape(n_sample, D_MODEL)
    outs = {k: [] for k in ("kp", "vp", "lfp", "ks", "vs", "lfs", "gvs")}
    for l in range(depth):
        win_b = jnp.pad(w_in[l], ((0, 0), (0, W_IN_COLS - w_in.shape[2]))).astype(BF16)
        bf_pad = jnp.pad(b_f[l], (0, LANES - N_B_HEADS)).reshape(1, LANES)
        nm = norm_mix[l].reshape(1, D_MODEL)
        nav = norm_av[l].reshape(1, D_A)
        reps = MLP_CHUNK // dseq
        ws_s = jnp.tile(w_s[l][:, :dseq, :dseq], (1, reps, reps))
        bst_p = b_s[l].T
        bst_s = jnp.tile(b_s[l][:, :dseq].T, (reps, 1))

        oa_p, q_p, k_p, v_p, kb_p, vb_p, lf_p, _ = _in_projection(
            xp, nm, win_b, bf_pad, nav, w_s[l], bst_p, tm=TM_PROJ, period=MLP_CHUNK)
        lf_p3 = lf_p.reshape(bsz, seq, N_B_HEADS)
        c_p = _cumsum_time(lf_p3.transpose(0, 2, 1))
        ob_p = _attention_prompt(q_p.reshape(bsz, seq, D_B), kb_p.reshape(bsz, seq, D_B),
                                 vb_p.reshape(bsz, seq, D_B), _head_pairs(c_p))

        oa_s, q_s, k_s, v_s, kb_s, vb_s, lf_s, vn_s = _in_projection(
            xs, nm, win_b, bf_pad, nav, ws_s, bst_s, tm=n_sample, period=dseq)
        lf_s3 = lf_s.reshape(dbsz, dseq, N_B_HEADS)
        t_all = past + dseq
        t_padded = -(-t_all // CS_BLK) * CS_BLK
        lf_all = jnp.concatenate([cache_logf[l].astype(F32), lf_s3], axis=1).transpose(0, 2, 1)
        c_all = _cumsum_time(jnp.pad(lf_all, ((0, 0), (0, 0), (0, t_padded - t_all))))
        ob_s = _attention_sample(q_s.reshape(dbsz, dseq, D_B),
                                 cache_k[l].reshape(dbsz, past, D_B), cache_v[l].reshape(dbsz, past, D_B),
                                 kb_s.reshape(dbsz, dseq, D_B), vb_s.reshape(dbsz, dseq, D_B),
                                 _head_pairs(c_all[:, :, :past]), _head_pairs(c_all[:, :, past:t_all]))

        pad_rows = lambda a: jnp.pad(a, ((0, TM_TOK - n_sample), (0, 0)))
        wr_b = jnp.pad(w_router[l], ((0, 0), (0, LANES - N_EXPERTS))).astype(BF16)
        br_pad = jnp.pad(b_router[l], (0, LANES - N_EXPERTS)).reshape(1, LANES)
        x2, h3, tidx, tw, rank, cnt = _tail(
            oa_p, ob_p.reshape(n_prompt, D_B), xp, pad_rows(oa_s), pad_rows(ob_s.reshape(n_sample, D_B)),
            pad_rows(xs), norm_out_a[l].reshape(1, D_A), norm_out_b[l].reshape(1, D_B), w_out[l].astype(BF16),
            norm_ffn[l].reshape(1, D_MODEL), wr_b, br_pad)
        counts = cnt[0, :N_EXPERTS].astype(I32)
        slot, te, n_used, pad_start, pad_len = _routing_tables(tidx[:, :TOP_K], rank[:, :TOP_K], counts, n_tiles_moe)
        xsort = _dispatch(pad_start, pad_len, n_used, slot, h3, n_slots)
        ysort = _moe(te, n_used, xsort, w1[l], b1[l].reshape(N_EXPERTS, 1, 2 * D_FF), w2[l],
                     b2[l].reshape(N_EXPERTS, 1, D_MODEL))
        y_all = _combine(slot, ysort, x2, tw, norm_final.reshape(1, D_MODEL))
        xp = y_all[:n_prompt]
        xs = y_all[n_prompt:n_prompt + n_sample]

        outs["kp"].append(k_p.reshape(bsz, seq, N_B_HEADS, HEAD_DIM))
        outs["vp"].append(v_p.reshape(bsz, seq, N_B_HEADS, HEAD_DIM))
        outs["lfp"].append(lf_p3)
        outs["ks"].append(k_s.reshape(dbsz, dseq, N_B_HEADS, HEAD_DIM))
        outs["vs"].append(v_s.reshape(dbsz, dseq, N_B_HEADS, HEAD_DIM))
        outs["lfs"].append(lf_s3)
        outs["gvs"].append(vn_s.reshape(dbsz, dseq, D_A))

    y_prompt = xp.reshape(bsz, seq, D_MODEL)
    y_sample = xs.reshape(dbsz, dseq, D_MODEL)
    st = jnp.stack
    return (y_prompt, y_sample, st(outs["kp"]), st(outs["vp"]), st(outs["lfp"]),
            st(outs["ks"]), st(outs["vs"]), st(outs["lfs"]), st(outs["gvs"]))
```

```python
import functools

import jax
import jax.numpy as jnp
import numpy as np
from jax import lax
from jax.experimental import pallas as pl
from jax.experimental.pallas import tpu as pltpu

F32 = jnp.float32
BF16 = jnp.bfloat16
I32 = jnp.int32

D_MODEL = 1024
CHUNK = 64
MLP_CHUNK = 128
D_A = 512
A_GROUPS = 4
A_GROUP_CH = D_A // A_GROUPS
D_B = 512
N_B_HEADS = 8
HEAD_DIM = D_B // N_B_HEADS
N_EXPERTS = 32
TOP_K = 4
D_FF = D_MODEL
SWIGLU_ALPHA = 1.702
SWIGLU_LIMIT = 7.0
RMS_EPS = 1e-6

LANES = 128
SUBLANES = 8
ROW_TILES = D_MODEL // LANES
HEADS_PER_BLOCK = LANES // HEAD_DIM
N_HEAD_BLOCKS = N_B_HEADS // HEADS_PER_BLOCK
QKV_OFF = 2 * D_A
F_OFF = 2 * D_A + 3 * D_B
W_IN_COLS = F_OFF + LANES

TM_PROJ = 512
TM_TOK = 256
TM_MOE = 256
TQ = 256
TKB = 1024
KCH = 1024
BIAS_PIECES = 3
CS_BLK = 256
FF_BLK = 512
VMEM_LIMIT = 48 * 1024 * 1024


def _cparams(sem, vmem=VMEM_LIMIT):
    return pltpu.CompilerParams(dimension_semantics=sem, vmem_limit_bytes=vmem)


def _gelu(x):
    return 0.5 * x * (1.0 + lax.erf(x * 0.7071067811865476))


def _log_sigmoid(x):
    return jnp.minimum(x, 0.0) - jnp.log1p(jnp.exp(-jnp.abs(x)))


def _sigmoid(x):
    return 1.0 / (1.0 + jnp.exp(-x))


def _rms(x):
    return x * lax.rsqrt(jnp.mean(x * x, axis=-1, keepdims=True) + RMS_EPS)


def _tile_rows(first_row, n_rows):
    return pl.ds(pl.multiple_of(first_row * ROW_TILES, ROW_TILES), n_rows * ROW_TILES)


def _lane_tile(j, n_rows, first_row=0):
    return pl.ds(first_row * ROW_TILES + j, n_rows, stride=ROW_TILES)


def _div_pow2(x, d):
    assert d & (d - 1) == 0
    return jnp.right_shift(x, d.bit_length() - 1)


def _mod_pow2(x, d):
    assert d & (d - 1) == 0
    return jnp.bitwise_and(x, d - 1)


def _inproj_kernel(x_ref, nm_ref, win_ref, bf_ref, nav_ref, ws_ref, bst_ref,
                   outa_ref, q_ref, k_ref, v_ref, kb_ref, vb_ref, logf_ref, vn_ref, *, period):
    tm = x_ref.shape[0]
    hn = (_rms(x_ref[...]) * nm_ref[...]).astype(BF16)

    def proj(lo, width):
        return jnp.dot(hn, win_ref[:, lo:lo + width], preferred_element_type=F32)

    u = _gelu(proj(0, D_A))
    vn = _rms(_gelu(proj(D_A, D_A))) * nav_ref[...]
    vn_ref[...] = vn
    q_ref[...] = (proj(QKV_OFF, D_B) * (HEAD_DIM ** -0.5)).astype(BF16)
    kk = proj(QKV_OFF + D_B, D_B)
    k_ref[...] = kk
    kb_ref[...] = kk.astype(BF16)
    vv = proj(QKV_OFF + 2 * D_B, D_B)
    v_ref[...] = vv
    vb_ref[...] = vv.astype(BF16)
    fl = proj(F_OFF, LANES) + bf_ref[...]
    logf_ref[...] = _log_sigmoid(fl)[:, :N_B_HEADS]

    vnb = vn.astype(BF16)
    ri = lax.broadcasted_iota(I32, (MLP_CHUNK, MLP_CHUNK), 0)
    ci = lax.broadcasted_iota(I32, (MLP_CHUNK, MLP_CHUNK), 1)
    same_stream = _div_pow2(ri, period) == _div_pow2(ci, period)
    visible = _div_pow2(_mod_pow2(ci, period), CHUNK) <= _div_pow2(_mod_pow2(ri, period), CHUNK)
    mask = jnp.logical_and(same_stream, visible)
    for g in range(A_GROUPS):
        w = jnp.where(mask, ws_ref[g], 0.0).astype(BF16)
        bcol = bst_ref[:, g:g + 1]
        cols = slice(g * A_GROUP_CH, (g + 1) * A_GROUP_CH)
        for c in range(tm // MLP_CHUNK):
            rows = slice(c * MLP_CHUNK, (c + 1) * MLP_CHUNK)
            mixed = jnp.dot(w, vnb[rows, cols], preferred_element_type=F32) + bcol
            outa_ref[rows, cols] = u[rows, cols] * mixed


def _in_projection(x2d, norm_mix, win_b, bf_pad, norm_av, ws_tiled, bst_tiled, *, tm, period):
    rows = x2d.shape[0]
    row_blk = lambda w: pl.BlockSpec((tm, w), lambda i: (i, 0))
    full = lambda shape: pl.BlockSpec(shape, lambda i: (0,) * len(shape))
    out_shape = (
        jax.ShapeDtypeStruct((rows, D_A), F32),
        jax.ShapeDtypeStruct((rows, D_B), BF16),
        jax.ShapeDtypeStruct((rows, D_B), F32),
        jax.ShapeDtypeStruct((rows, D_B), F32),
        jax.ShapeDtypeStruct((rows, D_B), BF16),
        jax.ShapeDtypeStruct((rows, D_B), BF16),
        jax.ShapeDtypeStruct((rows, N_B_HEADS), F32),
        jax.ShapeDtypeStruct((rows, D_A), F32),
    )
    return pl.pallas_call(
        functools.partial(_inproj_kernel, period=period),
        grid=(rows // tm,),
        in_specs=[row_blk(D_MODEL), full((1, D_MODEL)), full((D_MODEL, W_IN_COLS)), full((1, LANES)),
                  full((1, D_A)), full((A_GROUPS, MLP_CHUNK, MLP_CHUNK)), full((MLP_CHUNK, A_GROUPS))],
        out_specs=(row_blk(D_A), row_blk(D_B), row_blk(D_B), row_blk(D_B), row_blk(D_B), row_blk(D_B),
                   row_blk(N_B_HEADS), row_blk(D_A)),
        out_shape=out_shape,
        compiler_params=_cparams(("arbitrary",)),
        name="in_projection",
    )(x2d, norm_mix, win_b, bf_pad, norm_av, ws_tiled, bst_tiled)


def _split_bf16(x):
    x1 = x.astype(BF16)
    r1 = x - x1.astype(F32)
    x2 = r1.astype(BF16)
    x3 = (r1 - x2.astype(F32)).astype(BF16)
    return x1, x2, x3


def _cumsum_kernel(x_ref, o_ref, *piece_refs):
    t = x_ref.shape[2]
    ri = lax.broadcasted_iota(I32, (CS_BLK, CS_BLK), 0)
    ci = lax.broadcasted_iota(I32, (CS_BLK, CS_BLK), 1)
    upper = jnp.where(ri <= ci, 1.0, 0.0).astype(BF16)
    carry = jnp.zeros((N_B_HEADS, 1), F32)
    for c in range(t // CS_BLK):
        cols = slice(c * CS_BLK, (c + 1) * CS_BLK)
        s = sum(jnp.dot(piece, upper, preferred_element_type=F32) for piece in _split_bf16(x_ref[0, :, cols])) + carry
        o_ref[0, :, cols] = s
        for ref, piece in zip(piece_refs, _split_bf16(-s)):
            ref[0, :, cols] = piece.astype(F32)
        carry = s[:, CS_BLK - 1:CS_BLK]


def _cumsum_time(x_bht):
    b, h, t = x_bht.shape
    blk = pl.BlockSpec((1, h, t), lambda i: (i, 0, 0))
    return pl.pallas_call(
        _cumsum_kernel,
        grid=(b,),
        in_specs=[blk],
        out_specs=(blk,) * (1 + BIAS_PIECES),
        out_shape=(jax.ShapeDtypeStruct((b, h, t), F32),) * (1 + BIAS_PIECES),
        compiler_params=_cparams(("arbitrary",)),
        name="cumsum_logf",
    )(x_bht)


def _head_select(width=LANES):
    lane = lax.broadcasted_iota(I32, (1, width), 1)
    return [jnp.logical_and(lane >= h * HEAD_DIM, lane < (h + 1) * HEAD_DIM) for h in range(HEADS_PER_BLOCK)]


def _scores(qh, k2):
    return lax.dot_general(qh, k2, (((1,), (1,)), ((), ())), preferred_element_type=F32)


def _attn_prompt_kernel(q_ref, k_ref, vt_ref, cx_ref, o_ref, m_sc, l_sc, acc_sc):
    i = pl.program_id(2)
    lane = lax.broadcasted_iota(I32, (1, LANES), 1)
    q2 = q_ref[0]
    qaug = []
    for h in range(HEADS_PER_BLOCK):
        head = jnp.logical_and(lane >= h * HEAD_DIM, lane < (h + 1) * HEAD_DIM)
        pick = jnp.logical_and(lane >= h * BIAS_PIECES, lane < (h + 1) * BIAS_PIECES)
        head_row = jnp.where(head, 1.0, 0.0).astype(BF16)
        pick_row = jnp.where(pick, 1.0, 0.0).astype(BF16)
        qaug.append(jnp.concatenate([q2 * head_row, jnp.broadcast_to(pick_row, q2.shape)], axis=1))

    m_sc[...] = jnp.full(m_sc.shape, -jnp.inf, F32)
    l_sc[...] = jnp.zeros(l_sc.shape, F32)
    acc_sc[...] = jnp.zeros(acc_sc.shape, F32)

    def step(k0, width, masked):
        m = [m_sc[h] for h in range(HEADS_PER_BLOCK)]
        l = [l_sc[h] for h in range(HEADS_PER_BLOCK)]
        acc = [acc_sc[h] for h in range(HEADS_PER_BLOCK)]
        kch = min(KCH, width)
        for c in range(width // kch):
            kc = pl.multiple_of(k0 + c * kch, kch)
            kk = jnp.concatenate([k_ref[0, pl.ds(kc, kch), :], cx_ref[0, 0, pl.ds(kc, kch), :]], axis=1)
            for h in range(HEADS_PER_BLOCK):
                st = lax.dot_general(kk, qaug[h], (((1,), (1,)), ((), ())), preferred_element_type=F32)
                if masked:
                    key = lax.broadcasted_iota(I32, (kch, TQ), 0) + c * kch
                    qry = lax.broadcasted_iota(I32, (kch, TQ), 1)
                    st = jnp.where(key <= qry, st, -jnp.inf)
                m_new = jnp.maximum(m[h], jnp.max(st, axis=0, keepdims=True))
                a = jnp.exp(m[h] - m_new)
                p = jnp.exp(st - m_new)
                l[h] = a * l[h] + jnp.sum(p, axis=0, keepdims=True)
                m[h] = m_new
                vt = vt_ref[0, h * HEAD_DIM:(h + 1) * HEAD_DIM, pl.ds(kc, kch)]
                acc[h] = a * acc[h] + jnp.dot(vt, p.astype(BF16), preferred_element_type=F32)
        for h in range(HEADS_PER_BLOCK):
            m_sc[h] = m[h]
            l_sc[h] = l[h]
            acc_sc[h] = acc[h]

    n_big = i // (TKB // TQ)
    n_small = i % (TKB // TQ)

    def big(j, carry):
        step(pl.multiple_of(j * TKB, TKB), TKB, False)
        return carry

    def small(r, carry):
        step(pl.multiple_of(n_big * TKB + r * TQ, TQ), TQ, False)
        return carry

    lax.fori_loop(0, n_big, big, 0)
    lax.fori_loop(0, n_small, small, 0)
    step(pl.multiple_of(i * TQ, TQ), TQ, True)
    out_t = jnp.concatenate([acc_sc[h] / l_sc[h] for h in range(HEADS_PER_BLOCK)], axis=0)
    o_ref[0] = out_t.T


def _attention_prompt(q, kb, vt, cx):
    b, s, _ = q.shape
    nq = s // TQ
    return pl.pallas_call(
        _attn_prompt_kernel,
        grid=(b, N_HEAD_BLOCKS, nq),
        in_specs=[pl.BlockSpec((1, TQ, LANES), lambda bi, hp, i: (bi, i, hp)),
                  pl.BlockSpec((1, s, LANES), lambda bi, hp, i: (bi, 0, hp)),
                  pl.BlockSpec((1, LANES, s), lambda bi, hp, i: (bi, hp, 0)),
                  pl.BlockSpec((1, 1, s, LANES), lambda bi, hp, i: (bi, hp, 0, 0))],
        out_specs=pl.BlockSpec((1, TQ, LANES), lambda bi, hp, i: (bi, i, hp)),
        out_shape=jax.ShapeDtypeStruct((b, s, D_B), F32),
        scratch_shapes=[pltpu.VMEM((HEADS_PER_BLOCK, 1, TQ), F32),
                        pltpu.VMEM((HEADS_PER_BLOCK, 1, TQ), F32),
                        pltpu.VMEM((HEADS_PER_BLOCK, HEAD_DIM, TQ), F32)],
        compiler_params=_cparams(("arbitrary", "arbitrary", "arbitrary")),
        name="fox_attention_prompt",
    )(q, kb, vt, cx)


def _attn_sample_kernel(q_ref, kc_ref, vc_ref, kn_ref, vn_ref, cc_ref, cn_ref, o_ref):
    t = q_ref.shape[1]
    sel = _head_select()
    q2 = q_ref[0]
    zero = jnp.zeros_like(q2)
    kc = kc_ref[0].astype(BF16)
    vc = vc_ref[0].astype(BF16)
    kn = kn_ref[0]
    vn = vn_ref[0]
    row = lax.broadcasted_iota(I32, (t, t), 0)
    col = lax.broadcasted_iota(I32, (t, t), 1)
    outs = []
    for h in range(HEADS_PER_BLOCK):
        qh = jnp.where(sel[h], q2, zero)
        cbase = cn_ref[0, 0, h:h + 1, 0:1]
        s_c = _scores(qh, kc) + (cbase - cc_ref[0, 0, h:h + 1, :])
        s_n = _scores(qh, kn) + (cbase - cn_ref[0, 0, h:h + 1, :])
        s_n = jnp.where(col <= row, s_n, -jnp.inf)
        m = jnp.maximum(jnp.max(s_c, axis=-1, keepdims=True), jnp.max(s_n, axis=-1, keepdims=True))
        p_c = jnp.exp(s_c - m)
        p_n = jnp.exp(s_n - m)
        l = jnp.sum(p_c, axis=-1, keepdims=True) + jnp.sum(p_n, axis=-1, keepdims=True)
        o = (jnp.dot(p_c.astype(BF16), vc, preferred_element_type=F32)
             + jnp.dot(p_n.astype(BF16), vn, preferred_element_type=F32))
        outs.append(o / l)
    o_ref[0] = jnp.where(sel[0], outs[0], outs[1])


def _attention_sample(q, cache_k, cache_v, kb, vb, c_cache, c_new):
    b, t, _ = q.shape
    p = cache_k.shape[1]
    newblk = pl.BlockSpec((1, t, LANES), lambda bi, hp: (bi, 0, hp))
    cacheblk = pl.BlockSpec((1, p, LANES), lambda bi, hp: (bi, 0, hp))
    return pl.pallas_call(
        _attn_sample_kernel,
        grid=(b, N_HEAD_BLOCKS),
        in_specs=[newblk, cacheblk, cacheblk, newblk, newblk,
                  pl.BlockSpec((1, 1, HEADS_PER_BLOCK, p), lambda bi, hp: (bi, hp, 0, 0)),
                  pl.BlockSpec((1, 1, HEADS_PER_BLOCK, t), lambda bi, hp: (bi, hp, 0, 0))],
        out_specs=newblk,
        out_shape=jax.ShapeDtypeStruct((b, t, D_B), F32),
        compiler_params=_cparams(("arbitrary", "arbitrary")),
        name="fox_attention_sample",
    )(q, cache_k, cache_v, kb, vb, c_cache, c_new)


def _tail_kernel(oap_ref, obp_ref, xp_ref, oas_ref, obs_ref, xs_ref, noa_ref, nob_ref, wout_ref,
                 nffn_ref, wr_ref, br_ref,
                 x2_ref, h3_ref, tidx_ref, tw_ref, rank_ref, cnt_ref, carry_sc):
    i = pl.program_id(0)
    is_sample = i == pl.num_programs(0) - 1
    oa = jnp.where(is_sample, oas_ref[...], oap_ref[...])
    ob = jnp.where(is_sample, obs_ref[...], obp_ref[...])
    x = jnp.where(is_sample, xs_ref[...], xp_ref[...])

    ma = (_rms(oa) * noa_ref[...]).astype(BF16)
    mb = (_rms(ob) * nob_ref[...]).astype(BF16)
    x2 = x + (jnp.dot(ma, wout_ref[:D_A, :], preferred_element_type=F32)
              + jnp.dot(mb, wout_ref[D_A:, :], preferred_element_type=F32))
    x2_ref[...] = x2
    h = _rms(x2) * nffn_ref[...]
    for j in range(ROW_TILES):
        h3_ref[_lane_tile(j, TM_TOK), :] = h[:, j * LANES:(j + 1) * LANES]

    lane = lax.broadcasted_iota(I32, (TM_TOK, LANES), 1)
    logits = jnp.dot(h.astype(BF16), wr_ref[...], preferred_element_type=F32) + br_ref[...]
    logits = jnp.where(lane < N_EXPERTS, logits, -jnp.inf)

    lane_f = lane.astype(F32)
    vals, idxs, hots = [], [], []
    work = logits
    for _ in range(TOP_K):
        mx = jnp.max(work, axis=-1, keepdims=True)
        idx = jnp.min(jnp.where(work == mx, lane_f, float(LANES)), axis=-1, keepdims=True)
        hot = lane_f == idx
        vals.append(mx)
        idxs.append(idx.astype(I32))
        hots.append(hot)
        work = jnp.where(hot, -jnp.inf, work)
    exps = [jnp.exp(v - vals[0]) for v in vals]
    denom = exps[0] + exps[1] + exps[2] + exps[3]

    @pl.when(i == 0)
    def _():
        carry_sc[...] = jnp.zeros(carry_sc.shape, F32)

    assigned = jnp.zeros((TM_TOK, LANES), F32)
    for hot in hots:
        assigned = assigned + jnp.where(hot, 1.0, 0.0)
    ri = lax.broadcasted_iota(I32, (TM_TOK, TM_TOK), 0)
    ci = lax.broadcasted_iota(I32, (TM_TOK, TM_TOK), 1)
    earlier = jnp.where(ci < ri, 1.0, 0.0).astype(BF16)
    before = jnp.dot(earlier, assigned.astype(BF16), preferred_element_type=F32) + carry_sc[...]

    tidx = jnp.zeros((TM_TOK, LANES), I32)
    tw = jnp.zeros((TM_TOK, LANES), F32)
    rank = jnp.zeros((TM_TOK, LANES), I32)
    for k in range(TOP_K):
        rk = jnp.sum(jnp.where(hots[k], before, 0.0), axis=-1, keepdims=True).astype(I32)
        tidx = jnp.where(lane == k, idxs[k], tidx)
        tw = jnp.where(lane == k, exps[k] / denom, tw)
        rank = jnp.where(lane == k, rk, rank)
    tidx_ref[...] = tidx
    tw_ref[...] = tw
    rank_ref[...] = rank
    total = carry_sc[...] + jnp.sum(assigned, axis=0, keepdims=True)
    carry_sc[...] = total
    cnt_ref[...] = jnp.broadcast_to(total, cnt_ref.shape)


def _tail(oa_p, ob_p, x_p, oa_s, ob_s, x_s, noa, nob, wout_b, nffn, wr_b, br_pad):
    n_prompt_tiles = x_p.shape[0] // TM_TOK
    n_tiles = n_prompt_tiles + 1
    n_pad = n_tiles * TM_TOK
    pblk = lambda w: pl.BlockSpec((TM_TOK, w), lambda i: (jnp.minimum(i, n_prompt_tiles - 1), 0))
    full = lambda shape: pl.BlockSpec(shape, lambda i: (0,) * len(shape))
    oblk = lambda w: pl.BlockSpec((TM_TOK, w), lambda i: (i, 0))
    out_shape = (
        jax.ShapeDtypeStruct((n_pad, D_MODEL), F32),
        jax.ShapeDtypeStruct((n_pad * ROW_TILES, LANES), F32),
        jax.ShapeDtypeStruct((n_pad, LANES), I32),
        jax.ShapeDtypeStruct((n_pad, LANES), F32),
        jax.ShapeDtypeStruct((n_pad, LANES), I32),
        jax.ShapeDtypeStruct((SUBLANES, LANES), F32),
    )
    return pl.pallas_call(
        _tail_kernel,
        grid=(n_tiles,),
        in_specs=[pblk(D_A), pblk(D_B), pblk(D_MODEL), full((TM_TOK, D_A)), full((TM_TOK, D_B)),
                  full((TM_TOK, D_MODEL)), full((1, D_A)), full((1, D_B)), full((D_MODEL, D_MODEL)),
                  full((1, D_MODEL)), full((D_MODEL, LANES)), full((1, LANES))],
        out_specs=(oblk(D_MODEL), pl.BlockSpec((TM_TOK * ROW_TILES, LANES), lambda i: (i, 0)),
                   oblk(LANES), oblk(LANES), oblk(LANES), full((SUBLANES, LANES))),
        out_shape=out_shape,
        scratch_shapes=[pltpu.VMEM((1, LANES), F32)],
        compiler_params=_cparams(("arbitrary",)),
        name="outproj_router",
    )(oa_p, ob_p, x_p, oa_s, ob_s, x_s, noa, nob, wout_b, nffn, wr_b, br_pad)


def _row_copy(src, src_row, dst, dst_row, sem):
    return pltpu.make_async_copy(src.at[_tile_rows(src_row, 1)], dst.at[_tile_rows(dst_row, 1)], sem)


def _dispatch_kernel(pad_start_ref, pad_len_ref, nu_ref, slot_hbm, h3_ref, xs_hbm, slot_sm, zblk, sem, ssem):
    i = pl.program_id(0)
    n_assign = TM_TOK * TOP_K
    cp = pltpu.make_async_copy(slot_hbm.at[pl.ds(i * n_assign, n_assign)], slot_sm, ssem)
    cp.start()
    cp.wait()

    def issue(t, carry):
        for k in range(TOP_K):
            _row_copy(h3_ref, t, xs_hbm, slot_sm[t * TOP_K + k], sem).start(priority=k % 2)
        return carry

    def drain(t, carry):
        for k in range(TOP_K):
            _row_copy(h3_ref, t, xs_hbm, slot_sm[t * TOP_K + k], sem).wait()
        return carry

    lax.fori_loop(0, TM_TOK, issue, 0)
    lax.fori_loop(0, TM_TOK, drain, 0)

    @pl.when(i == pl.num_programs(0) - 1)
    def _():
        zblk[...] = jnp.zeros(zblk.shape, F32)
        for e in range(N_EXPERTS):
            start = pad_start_ref[e]
            n = pad_len_ref[e]

            def zissue(r, carry):
                _row_copy(zblk, 0, xs_hbm, start + r, sem).start()
                return carry

            def zdrain(r, carry):
                _row_copy(zblk, 0, xs_hbm, start + r, sem).wait()
                return carry

            lax.fori_loop(0, n, zissue, 0)
            lax.fori_loop(0, n, zdrain, 0)

        def tile_copy(r):
            return pltpu.make_async_copy(zblk, xs_hbm.at[_tile_rows(r * TM_MOE, TM_MOE)], sem)

        def tissue(r, carry):
            tile_copy(r).start()
            return carry

        def tdrain(r, carry):
            tile_copy(r).wait()
            return carry

        n_tiles_moe = xs_hbm.shape[0] // (TM_MOE * ROW_TILES)
        lax.fori_loop(nu_ref[0], n_tiles_moe, tissue, 0)
        lax.fori_loop(nu_ref[0], n_tiles_moe, tdrain, 0)


def _dispatch(pad_start, pad_len, n_used, slot_flat, h3, n_slots):
    n_tiles = h3.shape[0] // (TM_TOK * ROW_TILES)
    grid_spec = pltpu.PrefetchScalarGridSpec(
        num_scalar_prefetch=3,
        grid=(n_tiles,),
        in_specs=[pl.BlockSpec(memory_space=pl.ANY),
                  pl.BlockSpec((TM_TOK * ROW_TILES, LANES), lambda i, ps, pn, nu: (i, 0))],
        out_specs=pl.BlockSpec(memory_space=pl.ANY),
        scratch_shapes=[pltpu.SMEM((TM_TOK * TOP_K,), I32), pltpu.VMEM((TM_MOE * ROW_TILES, LANES), F32),
                        pltpu.SemaphoreType.DMA, pltpu.SemaphoreType.DMA],
    )
    return pl.pallas_call(
        _dispatch_kernel,
        grid_spec=grid_spec,
        out_shape=jax.ShapeDtypeStruct((n_slots * ROW_TILES, LANES), F32),
        compiler_params=_cparams(("arbitrary",)),
        name="moe_dispatch",
    )(pad_start, pad_len, n_used, slot_flat, h3)


def _moe_kernel(te_ref, nu_ref, xs_ref, w1_ref, b1_ref, w2_ref, b2_ref, ys_ref, w1b_sc, w2b_sc):
    i = pl.program_id(0)
    prev = te_ref[jnp.maximum(i - 1, 0)]
    new_expert = jnp.logical_or(i == 0, te_ref[i] != prev)

    @pl.when(new_expert)
    def _():
        blk = 128
        for r in range(D_MODEL // blk):
            w1b_sc[r * blk:(r + 1) * blk, :] = w1_ref[0, r * blk:(r + 1) * blk, :].astype(BF16)
        for r in range(D_FF // blk):
            w2b_sc[r * blk:(r + 1) * blk, :] = w2_ref[0, r * blk:(r + 1) * blk, :].astype(BF16)

    @pl.when(i < nu_ref[0])
    def _():
        xb = jnp.concatenate([xs_ref[_lane_tile(j, TM_MOE), :] for j in range(ROW_TILES)], axis=-1).astype(BF16)
        acc = jnp.zeros((TM_MOE, D_MODEL), F32)
        for c in range(D_FF // FF_BLK):
            lo = c * FF_BLK
            glu = jnp.dot(xb, w1b_sc[:, lo:lo + FF_BLK], preferred_element_type=F32) + b1_ref[0, :, lo:lo + FF_BLK]
            lin = (jnp.dot(xb, w1b_sc[:, D_FF + lo:D_FF + lo + FF_BLK], preferred_element_type=F32)
                   + b1_ref[0, :, D_FF + lo:D_FF + lo + FF_BLK])
            glu = jnp.minimum(glu, SWIGLU_LIMIT)
            lin = jnp.clip(lin, -SWIGLU_LIMIT, SWIGLU_LIMIT)
            act = glu * _sigmoid(SWIGLU_ALPHA * glu) * (lin + 1.0)
            acc = acc + jnp.dot(act.astype(BF16), w2b_sc[lo:lo + FF_BLK, :], preferred_element_type=F32)
        out = acc + b2_ref[0]
        for j in range(ROW_TILES):
            ys_ref[_lane_tile(j, TM_MOE), :] = out[:, j * LANES:(j + 1) * LANES]

    @pl.when(i >= nu_ref[0])
    def _():
        ys_ref[...] = jnp.zeros(ys_ref.shape, F32)


def _moe(tile_expert, n_used, xs, w1, b1, w2, b2):
    n_tiles = xs.shape[0] // (TM_MOE * ROW_TILES)
    exp_map = lambda i, te, nu: (te[i], 0, 0)
    grid_spec = pltpu.PrefetchScalarGridSpec(
        num_scalar_prefetch=2,
        grid=(n_tiles,),
        in_specs=[pl.BlockSpec((TM_MOE * ROW_TILES, LANES), lambda i, te, nu: (jnp.minimum(i, nu[0] - 1), 0)),
                  pl.BlockSpec((1, D_MODEL, 2 * D_FF), exp_map),
                  pl.BlockSpec((1, 1, 2 * D_FF), exp_map),
                  pl.BlockSpec((1, D_FF, D_MODEL), exp_map),
                  pl.BlockSpec((1, 1, D_MODEL), exp_map)],
        out_specs=pl.BlockSpec((TM_MOE * ROW_TILES, LANES), lambda i, te, nu: (i, 0)),
        scratch_shapes=[pltpu.VMEM((D_MODEL, 2 * D_FF), BF16), pltpu.VMEM((D_FF, D_MODEL), BF16)],
    )
    return pl.pallas_call(
        _moe_kernel,
        grid_spec=grid_spec,
        out_shape=jax.ShapeDtypeStruct(xs.shape, F32),
        compiler_params=_cparams(("arbitrary",), vmem=56 * 1024 * 1024),
        name="moe_experts",
    )(tile_expert, n_used, xs, w1, b1, w2, b2)


def _combine_kernel(slot_hbm, ys_hbm, x2_ref, tw_ref, nf_ref, yp_ref, ys_ref, slot_sm, buf, sem, ssem):
    i = pl.program_id(0)
    n_assign = TM_TOK * TOP_K
    cp = pltpu.make_async_copy(slot_hbm.at[pl.ds(i * n_assign, n_assign)], slot_sm, ssem)
    cp.start()
    cp.wait()

    def issue(t, carry):
        for k in range(TOP_K):
            _row_copy(ys_hbm, slot_sm[t * TOP_K + k], buf, k * TM_TOK + t, sem).start(priority=k % 2)
        return carry

    def drain(t, carry):
        for k in range(TOP_K):
            _row_copy(ys_hbm, slot_sm[t * TOP_K + k], buf, k * TM_TOK + t, sem).wait()
        return carry

    lax.fori_loop(0, TM_TOK, issue, 0)
    lax.fori_loop(0, TM_TOK, drain, 0)

    tw = tw_ref[...]
    parts = []
    for j in range(ROW_TILES):
        acc = jnp.zeros((TM_TOK, LANES), F32)
        for k in range(TOP_K):
            acc = acc + tw[:, k:k + 1] * buf[_lane_tile(j, TM_TOK, first_row=k * TM_TOK), :]
        parts.append(acc)
    y = _rms(x2_ref[...] + jnp.concatenate(parts, axis=-1)) * nf_ref[...]
    is_sample = i == pl.num_programs(0) - 1

    @pl.when(jnp.logical_not(is_sample))
    def _():
        yp_ref[...] = y

    @pl.when(is_sample)
    def _():
        ys_ref[...] = y


def _combine(slot_flat, ys, x2, tw, norm_final):
    n_tiles = x2.shape[0] // TM_TOK
    n_prompt_tiles = n_tiles - 1
    return pl.pallas_call(
        _combine_kernel,
        grid=(n_tiles,),
        in_specs=[pl.BlockSpec(memory_space=pl.ANY), pl.BlockSpec(memory_space=pl.ANY),
                  pl.BlockSpec((TM_TOK, D_MODEL), lambda i: (i, 0)),
                  pl.BlockSpec((TM_TOK, LANES), lambda i: (i, 0)),
                  pl.BlockSpec((1, D_MODEL), lambda i: (0, 0))],
        out_specs=(pl.BlockSpec((TM_TOK, D_MODEL), lambda i: (jnp.minimum(i, n_prompt_tiles - 1), 0)),
                   pl.BlockSpec((TM_TOK, D_MODEL), lambda i: (0, 0))),
        out_shape=(jax.ShapeDtypeStruct((n_prompt_tiles * TM_TOK, D_MODEL), F32),
                   jax.ShapeDtypeStruct((TM_TOK, D_MODEL), F32)),
        scratch_shapes=[pltpu.SMEM((TM_TOK * TOP_K,), I32),
                        pltpu.VMEM((TOP_K * TM_TOK * ROW_TILES, LANES), F32),
                        pltpu.SemaphoreType.DMA, pltpu.SemaphoreType.DMA],
        compiler_params=_cparams(("arbitrary",)),
        name="moe_combine",
    )(slot_flat, ys, x2, tw, norm_final)


def _head_pairs(c_bht):
    b, h, t = c_bht.shape
    return c_bht.reshape(b, N_HEAD_BLOCKS, HEADS_PER_BLOCK, t)


def _bias_lanes(pieces_bht):
    b, h, t = pieces_bht[0].shape
    x = jnp.stack(pieces_bht, axis=-1).astype(BF16)
    x = x.reshape(b, N_HEAD_BLOCKS, HEADS_PER_BLOCK, t, BIAS_PIECES).transpose(0, 1, 3, 2, 4)
    x = x.reshape(b, N_HEAD_BLOCKS, t, HEADS_PER_BLOCK * BIAS_PIECES)
    return jnp.pad(x, ((0, 0), (0, 0), (0, 0), (0, LANES - HEADS_PER_BLOCK * BIAS_PIECES)))


def _routing_tables(tidx, rank, counts, n_tiles_moe):
    padded = ((counts + TM_MOE - 1) // TM_MOE) * TM_MOE
    ends = jnp.cumsum(padded)
    offs = ends - padded
    experts = jnp.arange(N_EXPERTS, dtype=I32)
    off_of = jnp.sum(jnp.where(tidx[..., None] == experts, offs, 0), axis=-1)
    slot = (off_of + rank).reshape(-1).astype(I32)
    n_used = (ends[-1] // TM_MOE).astype(I32)
    tile_start = jnp.arange(n_tiles_moe, dtype=I32) * TM_MOE
    te = jnp.sum((tile_start[:, None] >= ends[None, :]).astype(I32), axis=-1)
    te_last = jnp.sum(((n_used - 1) * TM_MOE >= ends).astype(I32))
    te = jnp.minimum(jnp.where(jnp.arange(n_tiles_moe) < n_used, te, te_last), N_EXPERTS - 1).astype(I32)
    return slot, te, n_used.reshape(1), (offs + counts).astype(I32), (padded - counts).astype(I32)


def kernel(x_prompt, x_sample, cache_k, cache_v, cache_logf, norm_mix, w_in, b_f, norm_av, w_s, b_s,
           norm_out_a, norm_out_b, w_out, norm_ffn, w_router, b_router, w1, b1, w2, b2, norm_final):
    depth = w_in.shape[0]
    bsz, seq, _ = x_prompt.shape
    dbsz, dseq, _ = x_sample.shape
    past = cache_k.shape[2]
    n_prompt = bsz * seq
    n_sample = dbsz * dseq
    assert depth == 1, "the combine kernel applies the final norm, so it must follow the only layer"
    assert n_prompt % TM_PROJ == 0 and n_prompt % TM_TOK == 0 and seq % TQ == 0
    assert n_sample == MLP_CHUNK and MLP_CHUNK % dseq == 0 and n_sample <= TM_TOK
    n_pad = n_prompt + TM_TOK
    n_tiles_moe = (n_pad * TOP_K) // TM_MOE + N_EXPERTS
    n_slots = n_tiles_moe * TM_MOE

    xp = x_prompt.reshape(n_prompt, D_MODEL)
    xs = x_sample.reshape(n_sample, D_MODEL)
    outs = {k: [] for k in ("kp", "vp", "lfp", "ks", "vs", "lfs", "gvs")}
    for l in range(depth):
        win_b = jnp.pad(w_in[l], ((0, 0), (0, W_IN_COLS - w_in.shape[2]))).astype(BF16)
        bf_pad = jnp.pad(b_f[l], (0, LANES - N_B_HEADS)).reshape(1, LANES)
        nm = norm_mix[l].reshape(1, D_MODEL)
        nav = norm_av[l].reshape(1, D_A)
        reps = MLP_CHUNK // dseq
        ws_s = jnp.tile(w_s[l][:, :dseq, :dseq], (1, reps, reps))
        bst_p = b_s[l].T
        bst_s = jnp.tile(b_s[l][:, :dseq].T, (reps, 1))

        oa_p, q_p, k_p, v_p, kb_p, vb_p, lf_p, _ = _in_projection(
            xp, nm, win_b, bf_pad, nav, w_s[l], bst_p, tm=TM_PROJ, period=MLP_CHUNK)
        lf_p3 = lf_p.reshape(bsz, seq, N_B_HEADS)
        _, *neg_c_pieces = _cumsum_time(lf_p3.transpose(0, 2, 1))
        ob_p = _attention_prompt(q_p.reshape(bsz, seq, D_B), kb_p.reshape(bsz, seq, D_B),
                                 vb_p.reshape(bsz, seq, D_B).transpose(0, 2, 1), _bias_lanes(neg_c_pieces))

        oa_s, q_s, k_s, v_s, kb_s, vb_s, lf_s, vn_s = _in_projection(
            xs, nm, win_b, bf_pad, nav, ws_s, bst_s, tm=n_sample, period=dseq)
        lf_s3 = lf_s.reshape(dbsz, dseq, N_B_HEADS)
        t_all = past + dseq
        t_padded = -(-t_all // CS_BLK) * CS_BLK
        lf_all = jnp.concatenate([cache_logf[l].astype(F32), lf_s3], axis=1).transpose(0, 2, 1)
        c_all = _cumsum_time(jnp.pad(lf_all, ((0, 0), (0, 0), (0, t_padded - t_all))))[0]
        ob_s = _attention_sample(q_s.reshape(dbsz, dseq, D_B),
                                 cache_k[l].reshape(dbsz, past, D_B), cache_v[l].reshape(dbsz, past, D_B),
                                 kb_s.reshape(dbsz, dseq, D_B), vb_s.reshape(dbsz, dseq, D_B),
                                 _head_pairs(c_all[:, :, :past]), _head_pairs(c_all[:, :, past:t_all]))

        pad_rows = lambda a: jnp.pad(a, ((0, TM_TOK - n_sample), (0, 0)))
        wr_b = jnp.pad(w_router[l], ((0, 0), (0, LANES - N_EXPERTS))).astype(BF16)
        br_pad = jnp.pad(b_router[l], (0, LANES - N_EXPERTS)).reshape(1, LANES)
        x2, h3, tidx, tw, rank, cnt = _tail(
            oa_p, ob_p.reshape(n_prompt, D_B), xp, pad_rows(oa_s), pad_rows(ob_s.reshape(n_sample, D_B)),
            pad_rows(xs), norm_out_a[l].reshape(1, D_A), norm_out_b[l].reshape(1, D_B), w_out[l].astype(BF16),
            norm_ffn[l].reshape(1, D_MODEL), wr_b, br_pad)
        counts = cnt[0, :N_EXPERTS].astype(I32)
        slot, te, n_used, pad_start, pad_len = _routing_tables(tidx[:, :TOP_K], rank[:, :TOP_K], counts, n_tiles_moe)
        xsort = _dispatch(pad_start, pad_len, n_used, slot, h3, n_slots)
        ysort = _moe(te, n_used, xsort, w1[l], b1[l].reshape(N_EXPERTS, 1, 2 * D_FF), w2[l],
                     b2[l].reshape(N_EXPERTS, 1, D_MODEL))
        xp, y_tail = _combine(slot, ysort, x2, tw, norm_final.reshape(1, D_MODEL))
        xs = y_tail[:n_sample]

        outs["kp"].append(k_p.reshape(bsz, seq, N_B_HEADS, HEAD_DIM))
        outs["vp"].append(v_p.reshape(bsz, seq, N_B_HEADS, HEAD_DIM))
        outs["lfp"].append(lf_p3)
        outs["ks"].append(k_s.reshape(dbsz, dseq, N_B_HEADS, HEAD_DIM))
        outs["vs"].append(v_s.reshape(dbsz, dseq, N_B_HEADS, HEAD_DIM))
        outs["lfs"].append(lf_s3)
        outs["gvs"].append(vn_s.reshape(dbsz, dseq, D_A))

    y_prompt = xp.reshape(bsz, seq, D_MODEL)
    y_sample = xs.reshape(dbsz, dseq, D_MODEL)
    st = jnp.stack
    return (y_prompt, y_sample, st(outs["kp"]), st(outs["vp"]), st(outs["lfp"]),
            st(outs["ks"]), st(outs["vs"]), st(outs["lfs"]), st(outs["gvs"]))
```

```python
import functools

import jax
import jax.numpy as jnp
import numpy as np
from jax import lax
from jax.experimental import pallas as pl
from jax.experimental.pallas import tpu as pltpu

F32 = jnp.float32
BF16 = jnp.bfloat16
I32 = jnp.int32

D_MODEL = 1024
CHUNK = 64
MLP_CHUNK = 128
D_A = 512
A_GROUPS = 4
A_GROUP_CH = D_A // A_GROUPS
D_B = 512
N_B_HEADS = 8
HEAD_DIM = D_B // N_B_HEADS
N_EXPERTS = 32
TOP_K = 4
D_FF = D_MODEL
SWIGLU_ALPHA = 1.702
SWIGLU_LIMIT = 7.0
RMS_EPS = 1e-6

LANES = 128
SUBLANES = 8
ROW_TILES = D_MODEL // LANES
HEADS_PER_BLOCK = LANES // HEAD_DIM
N_HEAD_BLOCKS = N_B_HEADS // HEADS_PER_BLOCK
QKV_OFF = 2 * D_A
F_OFF = 2 * D_A + 3 * D_B
W_IN_COLS = F_OFF + LANES

TM_PROJ = 512
TM_TOK = 256
TM_MOE = 256
TQ = 256
TKB = 1024
KCH = 512
BIAS_PIECES = 3
CS_BLK = 256
FF_BLK = 512
VMEM_LIMIT = 48 * 1024 * 1024


def _cparams(sem, vmem=VMEM_LIMIT):
    return pltpu.CompilerParams(dimension_semantics=sem, vmem_limit_bytes=vmem)


def _gelu(x):
    return 0.5 * x * (1.0 + lax.erf(x * 0.7071067811865476))


def _log_sigmoid(x):
    return jnp.minimum(x, 0.0) - jnp.log1p(jnp.exp(-jnp.abs(x)))


def _sigmoid(x):
    return 1.0 / (1.0 + jnp.exp(-x))


def _rms(x):
    return x * lax.rsqrt(jnp.mean(x * x, axis=-1, keepdims=True) + RMS_EPS)


def _tile_rows(first_row, n_rows):
    return pl.ds(pl.multiple_of(first_row * ROW_TILES, ROW_TILES), n_rows * ROW_TILES)


def _lane_tile(j, n_rows, first_row=0):
    return pl.ds(first_row * ROW_TILES + j, n_rows, stride=ROW_TILES)


def _div_pow2(x, d):
    assert d & (d - 1) == 0
    return jnp.right_shift(x, d.bit_length() - 1)


def _mod_pow2(x, d):
    assert d & (d - 1) == 0
    return jnp.bitwise_and(x, d - 1)


def _inproj_kernel(x_ref, nm_ref, win_ref, bf_ref, nav_ref, ws_ref, bst_ref,
                   outa_ref, q_ref, k_ref, v_ref, kb_ref, vb_ref, logf_ref, vn_ref, *, period):
    tm = x_ref.shape[0]
    hn = (_rms(x_ref[...]) * nm_ref[...]).astype(BF16)

    def proj(lo, width):
        return jnp.dot(hn, win_ref[:, lo:lo + width], preferred_element_type=F32)

    u = _gelu(proj(0, D_A))
    vn = _rms(_gelu(proj(D_A, D_A))) * nav_ref[...]
    vn_ref[...] = vn
    q_ref[...] = (proj(QKV_OFF, D_B) * (HEAD_DIM ** -0.5)).astype(BF16)
    kk = proj(QKV_OFF + D_B, D_B)
    k_ref[...] = kk
    kb_ref[...] = kk.astype(BF16)
    vv = proj(QKV_OFF + 2 * D_B, D_B)
    v_ref[...] = vv
    vb_ref[...] = vv.astype(BF16)
    fl = proj(F_OFF, LANES) + bf_ref[...]
    logf_ref[...] = _log_sigmoid(fl)[:, :N_B_HEADS]

    vnb = vn.astype(BF16)
    ri = lax.broadcasted_iota(I32, (MLP_CHUNK, MLP_CHUNK), 0)
    ci = lax.broadcasted_iota(I32, (MLP_CHUNK, MLP_CHUNK), 1)
    same_stream = _div_pow2(ri, period) == _div_pow2(ci, period)
    visible = _div_pow2(_mod_pow2(ci, period), CHUNK) <= _div_pow2(_mod_pow2(ri, period), CHUNK)
    mask = jnp.logical_and(same_stream, visible)
    for g in range(A_GROUPS):
        w = jnp.where(mask, ws_ref[g], 0.0).astype(BF16)
        bcol = bst_ref[:, g:g + 1]
        cols = slice(g * A_GROUP_CH, (g + 1) * A_GROUP_CH)
        for c in range(tm // MLP_CHUNK):
            rows = slice(c * MLP_CHUNK, (c + 1) * MLP_CHUNK)
            mixed = jnp.dot(w, vnb[rows, cols], preferred_element_type=F32) + bcol
            outa_ref[rows, cols] = u[rows, cols] * mixed


def _in_projection(x2d, norm_mix, win_b, bf_pad, norm_av, ws_tiled, bst_tiled, *, tm, period):
    rows = x2d.shape[0]
    row_blk = lambda w: pl.BlockSpec((tm, w), lambda i: (i, 0))
    full = lambda shape: pl.BlockSpec(shape, lambda i: (0,) * len(shape))
    out_shape = (
        jax.ShapeDtypeStruct((rows, D_A), F32),
        jax.ShapeDtypeStruct((rows, D_B), BF16),
        jax.ShapeDtypeStruct((rows, D_B), F32),
        jax.ShapeDtypeStruct((rows, D_B), F32),
        jax.ShapeDtypeStruct((rows, D_B), BF16),
        jax.ShapeDtypeStruct((rows, D_B), BF16),
        jax.ShapeDtypeStruct((rows, N_B_HEADS), F32),
        jax.ShapeDtypeStruct((rows, D_A), F32),
    )
    return pl.pallas_call(
        functools.partial(_inproj_kernel, period=period),
        grid=(rows // tm,),
        in_specs=[row_blk(D_MODEL), full((1, D_MODEL)), full((D_MODEL, W_IN_COLS)), full((1, LANES)),
                  full((1, D_A)), full((A_GROUPS, MLP_CHUNK, MLP_CHUNK)), full((MLP_CHUNK, A_GROUPS))],
        out_specs=(row_blk(D_A), row_blk(D_B), row_blk(D_B), row_blk(D_B), row_blk(D_B), row_blk(D_B),
                   row_blk(N_B_HEADS), row_blk(D_A)),
        out_shape=out_shape,
        compiler_params=_cparams(("arbitrary",)),
        name="in_projection",
    )(x2d, norm_mix, win_b, bf_pad, norm_av, ws_tiled, bst_tiled)


def _split_bf16(x):
    x1 = x.astype(BF16)
    r1 = x - x1.astype(F32)
    x2 = r1.astype(BF16)
    x3 = (r1 - x2.astype(F32)).astype(BF16)
    return x1, x2, x3


def _cumsum_kernel(x_ref, o_ref, *piece_refs):
    t = x_ref.shape[2]
    ri = lax.broadcasted_iota(I32, (CS_BLK, CS_BLK), 0)
    ci = lax.broadcasted_iota(I32, (CS_BLK, CS_BLK), 1)
    upper = jnp.where(ri <= ci, 1.0, 0.0).astype(BF16)
    carry = jnp.zeros((N_B_HEADS, 1), F32)
    for c in range(t // CS_BLK):
        cols = slice(c * CS_BLK, (c + 1) * CS_BLK)
        s = sum(jnp.dot(piece, upper, preferred_element_type=F32) for piece in _split_bf16(x_ref[0, :, cols])) + carry
        o_ref[0, :, cols] = s
        for ref, piece in zip(piece_refs, _split_bf16(-s)):
            ref[0, :, cols] = piece.astype(F32)
        carry = s[:, CS_BLK - 1:CS_BLK]


def _cumsum_time(x_bht):
    b, h, t = x_bht.shape
    blk = pl.BlockSpec((1, h, t), lambda i: (i, 0, 0))
    return pl.pallas_call(
        _cumsum_kernel,
        grid=(b,),
        in_specs=[blk],
        out_specs=(blk,) * (1 + BIAS_PIECES),
        out_shape=(jax.ShapeDtypeStruct((b, h, t), F32),) * (1 + BIAS_PIECES),
        compiler_params=_cparams(("arbitrary",)),
        name="cumsum_logf",
    )(x_bht)


def _head_select(width=LANES):
    lane = lax.broadcasted_iota(I32, (1, width), 1)
    return [jnp.logical_and(lane >= h * HEAD_DIM, lane < (h + 1) * HEAD_DIM) for h in range(HEADS_PER_BLOCK)]


def _scores(qh, k2):
    return lax.dot_general(qh, k2, (((1,), (1,)), ((), ())), preferred_element_type=F32)


def _attn_prompt_kernel(q_ref, k_ref, vt_ref, cx_ref, o_ref, m_sc, l_sc, acc_sc, sa_sc, sb_sc):
    i = pl.program_id(2)
    lane = lax.broadcasted_iota(I32, (1, LANES), 1)
    q2 = q_ref[0]
    qaug = []
    for h in range(HEADS_PER_BLOCK):
        head = jnp.logical_and(lane >= h * HEAD_DIM, lane < (h + 1) * HEAD_DIM)
        pick = jnp.logical_and(lane >= h * BIAS_PIECES, lane < (h + 1) * BIAS_PIECES)
        head_row = jnp.where(head, 1.0, 0.0).astype(BF16)
        pick_row = jnp.where(pick, 1.0, 0.0).astype(BF16)
        qaug.append(jnp.concatenate([q2 * head_row, jnp.broadcast_to(pick_row, q2.shape)], axis=1))

    m_sc[...] = jnp.full(m_sc.shape, -jnp.inf, F32)
    l_sc[...] = jnp.zeros(l_sc.shape, F32)
    acc_sc[...] = jnp.zeros(acc_sc.shape, F32)

    def scores_into(c, s_sc):
        kc = pl.multiple_of(c * TQ, TQ)
        kk = jnp.concatenate([k_ref[0, pl.ds(kc, TQ), :], cx_ref[0, 0, pl.ds(kc, TQ), :]], axis=1)
        for h in range(HEADS_PER_BLOCK):
            s_sc[h] = lax.dot_general(kk, qaug[h], (((1,), (1,)), ((), ())), preferred_element_type=F32)

    def consume(c, s_sc, masked):
        kc = pl.multiple_of(c * TQ, TQ)
        for h in range(HEADS_PER_BLOCK):
            st = s_sc[h]
            if masked:
                key = lax.broadcasted_iota(I32, (TQ, TQ), 0)
                qry = lax.broadcasted_iota(I32, (TQ, TQ), 1)
                st = jnp.where(key <= qry, st, -jnp.inf)
            m_prev = m_sc[h]
            m_new = jnp.maximum(m_prev, jnp.max(st, axis=0, keepdims=True))
            a = jnp.exp(m_prev - m_new)
            p = jnp.exp(st - m_new)
            l_sc[h] = a * l_sc[h] + jnp.sum(p, axis=0, keepdims=True)
            m_sc[h] = m_new
            vt = vt_ref[0, h * HEAD_DIM:(h + 1) * HEAD_DIM, pl.ds(kc, TQ)]
            acc_sc[h] = a * acc_sc[h] + jnp.dot(vt, p.astype(BF16), preferred_element_type=F32)

    scores_into(0, sa_sc)

    def pair(j, carry):
        scores_into(2 * j + 1, sb_sc)
        consume(2 * j, sa_sc, False)
        scores_into(2 * j + 2, sa_sc)
        consume(2 * j + 1, sb_sc, False)
        return carry

    lax.fori_loop(0, i // 2, pair, 0)

    @pl.when(i % 2 == 0)
    def _():
        consume(i, sa_sc, True)

    @pl.when(i % 2 == 1)
    def _():
        scores_into(i, sb_sc)
        consume(i - 1, sa_sc, False)
        consume(i, sb_sc, True)

    out_t = jnp.concatenate([acc_sc[h] / l_sc[h] for h in range(HEADS_PER_BLOCK)], axis=0)
    o_ref[0] = out_t.T


def _attention_prompt(q, kb, vt, cx):
    b, s, _ = q.shape
    nq = s // TQ
    return pl.pallas_call(
        _attn_prompt_kernel,
        grid=(b, N_HEAD_BLOCKS, nq),
        in_specs=[pl.BlockSpec((1, TQ, LANES), lambda bi, hp, i: (bi, i, hp)),
                  pl.BlockSpec((1, s, LANES), lambda bi, hp, i: (bi, 0, hp)),
                  pl.BlockSpec((1, LANES, s), lambda bi, hp, i: (bi, hp, 0)),
                  pl.BlockSpec((1, 1, s, LANES), lambda bi, hp, i: (bi, hp, 0, 0))],
        out_specs=pl.BlockSpec((1, TQ, LANES), lambda bi, hp, i: (bi, i, hp)),
        out_shape=jax.ShapeDtypeStruct((b, s, D_B), F32),
        scratch_shapes=[pltpu.VMEM((HEADS_PER_BLOCK, 1, TQ), F32),
                        pltpu.VMEM((HEADS_PER_BLOCK, 1, TQ), F32),
                        pltpu.VMEM((HEADS_PER_BLOCK, HEAD_DIM, TQ), F32),
                        pltpu.VMEM((HEADS_PER_BLOCK, TQ, TQ), F32),
                        pltpu.VMEM((HEADS_PER_BLOCK, TQ, TQ), F32)],
        compiler_params=_cparams(("arbitrary", "arbitrary", "arbitrary")),
        name="fox_attention_prompt",
    )(q, kb, vt, cx)


def _attn_sample_kernel(q_ref, kc_ref, vc_ref, kn_ref, vn_ref, cc_ref, cn_ref, o_ref):
    t = q_ref.shape[1]
    sel = _head_select()
    q2 = q_ref[0]
    zero = jnp.zeros_like(q2)
    kc = kc_ref[0].astype(BF16)
    vc = vc_ref[0].astype(BF16)
    kn = kn_ref[0]
    vn = vn_ref[0]
    row = lax.broadcasted_iota(I32, (t, t), 0)
    col = lax.broadcasted_iota(I32, (t, t), 1)
    outs = []
    for h in range(HEADS_PER_BLOCK):
        qh = jnp.where(sel[h], q2, zero)
        cbase = cn_ref[0, 0, h:h + 1, 0:1]
        s_c = _scores(qh, kc) + (cbase - cc_ref[0, 0, h:h + 1, :])
        s_n = _scores(qh, kn) + (cbase - cn_ref[0, 0, h:h + 1, :])
        s_n = jnp.where(col <= row, s_n, -jnp.inf)
        m = jnp.maximum(jnp.max(s_c, axis=-1, keepdims=True), jnp.max(s_n, axis=-1, keepdims=True))
        p_c = jnp.exp(s_c - m)
        p_n = jnp.exp(s_n - m)
        l = jnp.sum(p_c, axis=-1, keepdims=True) + jnp.sum(p_n, axis=-1, keepdims=True)
        o = (jnp.dot(p_c.astype(BF16), vc, preferred_element_type=F32)
             + jnp.dot(p_n.astype(BF16), vn, preferred_element_type=F32))
        outs.append(o / l)
    o_ref[0] = jnp.where(sel[0], outs[0], outs[1])


def _attention_sample(q, cache_k, cache_v, kb, vb, c_cache, c_new):
    b, t, _ = q.shape
    p = cache_k.shape[1]
    newblk = pl.BlockSpec((1, t, LANES), lambda bi, hp: (bi, 0, hp))
    cacheblk = pl.BlockSpec((1, p, LANES), lambda bi, hp: (bi, 0, hp))
    return pl.pallas_call(
        _attn_sample_kernel,
        grid=(b, N_HEAD_BLOCKS),
        in_specs=[newblk, cacheblk, cacheblk, newblk, newblk,
                  pl.BlockSpec((1, 1, HEADS_PER_BLOCK, p), lambda bi, hp: (bi, hp, 0, 0)),
                  pl.BlockSpec((1, 1, HEADS_PER_BLOCK, t), lambda bi, hp: (bi, hp, 0, 0))],
        out_specs=newblk,
        out_shape=jax.ShapeDtypeStruct((b, t, D_B), F32),
        compiler_params=_cparams(("arbitrary", "arbitrary")),
        name="fox_attention_sample",
    )(q, cache_k, cache_v, kb, vb, c_cache, c_new)


def _tail_kernel(oap_ref, obp_ref, xp_ref, oas_ref, obs_ref, xs_ref, noa_ref, nob_ref, wout_ref,
                 nffn_ref, wr_ref, br_ref,
                 x2_ref, h3_ref, tidx_ref, tw_ref, rank_ref, cnt_ref, carry_sc):
    i = pl.program_id(0)
    is_sample = i == pl.num_programs(0) - 1
    oa = jnp.where(is_sample, oas_ref[...], oap_ref[...])
    ob = jnp.where(is_sample, obs_ref[...], obp_ref[...])
    x = jnp.where(is_sample, xs_ref[...], xp_ref[...])

    ma = (_rms(oa) * noa_ref[...]).astype(BF16)
    mb = (_rms(ob) * nob_ref[...]).astype(BF16)
    x2 = x + (jnp.dot(ma, wout_ref[:D_A, :], preferred_element_type=F32)
              + jnp.dot(mb, wout_ref[D_A:, :], preferred_element_type=F32))
    x2_ref[...] = x2
    h = _rms(x2) * nffn_ref[...]
    for j in range(ROW_TILES):
        h3_ref[_lane_tile(j, TM_TOK), :] = h[:, j * LANES:(j + 1) * LANES]

    lane = lax.broadcasted_iota(I32, (TM_TOK, LANES), 1)
    logits = jnp.dot(h.astype(BF16), wr_ref[...], preferred_element_type=F32) + br_ref[...]
    logits = jnp.where(lane < N_EXPERTS, logits, -jnp.inf)

    lane_f = lane.astype(F32)
    vals, idxs, hots = [], [], []
    work = logits
    for _ in range(TOP_K):
        mx = jnp.max(work, axis=-1, keepdims=True)
        idx = jnp.min(jnp.where(work == mx, lane_f, float(LANES)), axis=-1, keepdims=True)
        hot = lane_f == idx
        vals.append(mx)
        idxs.append(idx.astype(I32))
        hots.append(hot)
        work = jnp.where(hot, -jnp.inf, work)
    exps = [jnp.exp(v - vals[0]) for v in vals]
    denom = exps[0] + exps[1] + exps[2] + exps[3]

    @pl.when(i == 0)
    def _():
        carry_sc[...] = jnp.zeros(carry_sc.shape, F32)

    assigned = jnp.zeros((TM_TOK, LANES), F32)
    for hot in hots:
        assigned = assigned + jnp.where(hot, 1.0, 0.0)
    ri = lax.broadcasted_iota(I32, (TM_TOK, TM_TOK), 0)
    ci = lax.broadcasted_iota(I32, (TM_TOK, TM_TOK), 1)
    earlier = jnp.where(ci < ri, 1.0, 0.0).astype(BF16)
    before = jnp.dot(earlier, assigned.astype(BF16), preferred_element_type=F32) + carry_sc[...]

    tidx = jnp.zeros((TM_TOK, LANES), I32)
    tw = jnp.zeros((TM_TOK, LANES), F32)
    rank = jnp.zeros((TM_TOK, LANES), I32)
    for k in range(TOP_K):
        rk = jnp.sum(jnp.where(hots[k], before, 0.0), axis=-1, keepdims=True).astype(I32)
        tidx = jnp.where(lane == k, idxs[k], tidx)
        tw = jnp.where(lane == k, exps[k] / denom, tw)
        rank = jnp.where(lane == k, rk, rank)
    tidx_ref[...] = tidx
    tw_ref[...] = tw
    rank_ref[...] = rank
    total = carry_sc[...] + jnp.sum(assigned, axis=0, keepdims=True)
    carry_sc[...] = total
    cnt_ref[...] = jnp.broadcast_to(total, cnt_ref.shape)


def _tail(oa_p, ob_p, x_p, oa_s, ob_s, x_s, noa, nob, wout_b, nffn, wr_b, br_pad):
    n_prompt_tiles = x_p.shape[0] // TM_TOK
    n_tiles = n_prompt_tiles + 1
    n_pad = n_tiles * TM_TOK
    pblk = lambda w: pl.BlockSpec((TM_TOK, w), lambda i: (jnp.minimum(i, n_prompt_tiles - 1), 0))
    full = lambda shape: pl.BlockSpec(shape, lambda i: (0,) * len(shape))
    oblk = lambda w: pl.BlockSpec((TM_TOK, w), lambda i: (i, 0))
    out_shape = (
        jax.ShapeDtypeStruct((n_pad, D_MODEL), F32),
        jax.ShapeDtypeStruct((n_pad * ROW_TILES, LANES), F32),
        jax.ShapeDtypeStruct((n_pad, LANES), I32),
        jax.ShapeDtypeStruct((n_pad, LANES), F32),
        jax.ShapeDtypeStruct((n_pad, LANES), I32),
        jax.ShapeDtypeStruct((SUBLANES, LANES), F32),
    )
    return pl.pallas_call(
        _tail_kernel,
        grid=(n_tiles,),
        in_specs=[pblk(D_A), pblk(D_B), pblk(D_MODEL), full((TM_TOK, D_A)), full((TM_TOK, D_B)),
                  full((TM_TOK, D_MODEL)), full((1, D_A)), full((1, D_B)), full((D_MODEL, D_MODEL)),
                  full((1, D_MODEL)), full((D_MODEL, LANES)), full((1, LANES))],
        out_specs=(oblk(D_MODEL), pl.BlockSpec((TM_TOK * ROW_TILES, LANES), lambda i: (i, 0)),
                   oblk(LANES), oblk(LANES), oblk(LANES), full((SUBLANES, LANES))),
        out_shape=out_shape,
        scratch_shapes=[pltpu.VMEM((1, LANES), F32)],
        compiler_params=_cparams(("arbitrary",)),
        name="outproj_router",
    )(oa_p, ob_p, x_p, oa_s, ob_s, x_s, noa, nob, wout_b, nffn, wr_b, br_pad)


def _row_copy(src, src_row, dst, dst_row, sem):
    return pltpu.make_async_copy(src.at[_tile_rows(src_row, 1)], dst.at[_tile_rows(dst_row, 1)], sem)


N_ASSIGN = TM_TOK * TOP_K
SLOT_RING = 3
ISSUE_UNROLL = 4


def _slot_copy(slot_hbm, slot_sm, ssem, tile):
    place = tile % SLOT_RING
    return pltpu.make_async_copy(slot_hbm.at[pl.ds(pl.multiple_of(tile * N_ASSIGN, N_ASSIGN), N_ASSIGN)],
                                 slot_sm.at[pl.ds(pl.multiple_of(place * N_ASSIGN, N_ASSIGN), N_ASSIGN)],
                                 ssem.at[place])


def _for_each_assignment(slot_sm, tile, fn):
    base = (tile % SLOT_RING) * N_ASSIGN

    def body(g, carry):
        first = g * ISSUE_UNROLL
        slots = [slot_sm[base + first * TOP_K + a] for a in range(ISSUE_UNROLL * TOP_K)]
        for u in range(ISSUE_UNROLL):
            for k in range(TOP_K):
                fn(first + u, k, slots[u * TOP_K + k])
        return carry

    lax.fori_loop(0, TM_TOK // ISSUE_UNROLL, body, 0)


def _dispatch_kernel(pad_start_ref, pad_len_ref, nu_ref, slot_hbm, h3_hbm, xs_hbm, slot_sm, zblk, sem, ssem, zsem):
    i = pl.program_id(0)
    n = pl.num_programs(0)

    @pl.when(i == 0)
    def _():
        _slot_copy(slot_hbm, slot_sm, ssem, 0).start()

    @pl.when(i + 1 < n)
    def _():
        _slot_copy(slot_hbm, slot_sm, ssem, i + 1).start()

    _slot_copy(slot_hbm, slot_sm, ssem, i).wait()

    def row_copy(tile, t, slot):
        return _row_copy(h3_hbm, tile * TM_TOK + t, xs_hbm, slot, sem.at[tile % 2])

    _for_each_assignment(slot_sm, i, lambda t, k, slot: row_copy(i, t, slot).start(priority=k % 2))

    @pl.when(i > 0)
    def _():
        _for_each_assignment(slot_sm, i - 1, lambda t, k, slot: row_copy(i - 1, t, slot).wait())

    @pl.when(i == n - 1)
    def _():
        _for_each_assignment(slot_sm, i, lambda t, k, slot: row_copy(i, t, slot).wait())
        zblk[...] = jnp.zeros(zblk.shape, F32)
        for e in range(N_EXPERTS):
            start = pad_start_ref[e]
            n_pad_rows = pad_len_ref[e]

            def zissue(r, carry):
                _row_copy(zblk, 0, xs_hbm, start + r, zsem).start()
                return carry

            def zdrain(r, carry):
                _row_copy(zblk, 0, xs_hbm, start + r, zsem).wait()
                return carry

            lax.fori_loop(0, n_pad_rows, zissue, 0)
            lax.fori_loop(0, n_pad_rows, zdrain, 0)

        def tile_copy(r):
            return pltpu.make_async_copy(zblk, xs_hbm.at[_tile_rows(r * TM_MOE, TM_MOE)], zsem)

        def tissue(r, carry):
            tile_copy(r).start()
            return carry

        def tdrain(r, carry):
            tile_copy(r).wait()
            return carry

        n_tiles_moe = xs_hbm.shape[0] // (TM_MOE * ROW_TILES)
        lax.fori_loop(nu_ref[0], n_tiles_moe, tissue, 0)
        lax.fori_loop(nu_ref[0], n_tiles_moe, tdrain, 0)


def _dispatch(pad_start, pad_len, n_used, slot_flat, h3, n_slots):
    n_tiles = h3.shape[0] // (TM_TOK * ROW_TILES)
    grid_spec = pltpu.PrefetchScalarGridSpec(
        num_scalar_prefetch=3,
        grid=(n_tiles,),
        in_specs=[pl.BlockSpec(memory_space=pl.ANY), pl.BlockSpec(memory_space=pl.ANY)],
        out_specs=pl.BlockSpec(memory_space=pl.ANY),
        scratch_shapes=[pltpu.SMEM((SLOT_RING * N_ASSIGN,), I32), pltpu.VMEM((TM_MOE * ROW_TILES, LANES), F32),
                        pltpu.SemaphoreType.DMA((2,)), pltpu.SemaphoreType.DMA((SLOT_RING,)),
                        pltpu.SemaphoreType.DMA],
    )
    return pl.pallas_call(
        _dispatch_kernel,
        grid_spec=grid_spec,
        out_shape=jax.ShapeDtypeStruct((n_slots * ROW_TILES, LANES), F32),
        compiler_params=_cparams(("arbitrary",)),
        name="moe_dispatch",
    )(pad_start, pad_len, n_used, slot_flat, h3)


def _moe_kernel(te_ref, nu_ref, xs_ref, w1_ref, b1_ref, w2_ref, b2_ref, ys_ref, w1b_sc, w2b_sc):
    i = pl.program_id(0)
    prev = te_ref[jnp.maximum(i - 1, 0)]
    new_expert = jnp.logical_or(i == 0, te_ref[i] != prev)

    @pl.when(new_expert)
    def _():
        blk = 128
        for r in range(D_MODEL // blk):
            w1b_sc[r * blk:(r + 1) * blk, :] = w1_ref[0, r * blk:(r + 1) * blk, :].astype(BF16)
        for r in range(D_FF // blk):
            w2b_sc[r * blk:(r + 1) * blk, :] = w2_ref[0, r * blk:(r + 1) * blk, :].astype(BF16)

    @pl.when(i < nu_ref[0])
    def _():
        xb = jnp.concatenate([xs_ref[_lane_tile(j, TM_MOE), :] for j in range(ROW_TILES)], axis=-1).astype(BF16)
        acc = jnp.zeros((TM_MOE, D_MODEL), F32)
        for c in range(D_FF // FF_BLK):
            lo = c * FF_BLK
            glu = jnp.dot(xb, w1b_sc[:, lo:lo + FF_BLK], preferred_element_type=F32) + b1_ref[0, :, lo:lo + FF_BLK]
            lin = (jnp.dot(xb, w1b_sc[:, D_FF + lo:D_FF + lo + FF_BLK], preferred_element_type=F32)
                   + b1_ref[0, :, D_FF + lo:D_FF + lo + FF_BLK])
            glu = jnp.minimum(glu, SWIGLU_LIMIT)
            lin = jnp.clip(lin, -SWIGLU_LIMIT, SWIGLU_LIMIT)
            act = glu * _sigmoid(SWIGLU_ALPHA * glu) * (lin + 1.0)
            acc = acc + jnp.dot(act.astype(BF16), w2b_sc[lo:lo + FF_BLK, :], preferred_element_type=F32)
        out = acc + b2_ref[0]
        for j in range(ROW_TILES):
            ys_ref[_lane_tile(j, TM_MOE), :] = out[:, j * LANES:(j + 1) * LANES]

    @pl.when(i >= nu_ref[0])
    def _():
        ys_ref[...] = jnp.zeros(ys_ref.shape, F32)


def _moe(tile_expert, n_used, xs, w1, b1, w2, b2):
    n_tiles = xs.shape[0] // (TM_MOE * ROW_TILES)
    exp_map = lambda i, te, nu: (te[i], 0, 0)
    grid_spec = pltpu.PrefetchScalarGridSpec(
        num_scalar_prefetch=2,
        grid=(n_tiles,),
        in_specs=[pl.BlockSpec((TM_MOE * ROW_TILES, LANES), lambda i, te, nu: (jnp.minimum(i, nu[0] - 1), 0)),
                  pl.BlockSpec((1, D_MODEL, 2 * D_FF), exp_map),
                  pl.BlockSpec((1, 1, 2 * D_FF), exp_map),
                  pl.BlockSpec((1, D_FF, D_MODEL), exp_map),
                  pl.BlockSpec((1, 1, D_MODEL), exp_map)],
        out_specs=pl.BlockSpec((TM_MOE * ROW_TILES, LANES), lambda i, te, nu: (i, 0)),
        scratch_shapes=[pltpu.VMEM((D_MODEL, 2 * D_FF), BF16), pltpu.VMEM((D_FF, D_MODEL), BF16)],
    )
    return pl.pallas_call(
        _moe_kernel,
        grid_spec=grid_spec,
        out_shape=jax.ShapeDtypeStruct(xs.shape, F32),
        compiler_params=_cparams(("arbitrary",), vmem=56 * 1024 * 1024),
        name="moe_experts",
    )(tile_expert, n_used, xs, w1, b1, w2, b2)


def _combine_kernel(slot_hbm, ys_hbm, x2_ref, tw_ref, nf_ref, yp_ref, ys_ref, slot_sm, buf_a, buf_b, sem, ssem):
    i = pl.program_id(0)
    n = pl.num_programs(0)

    def row_copy(tile, buf, t, k, slot):
        return _row_copy(ys_hbm, slot, buf, k * TM_TOK + t, sem.at[tile % 2])

    def start_gather(tile, buf):
        _for_each_assignment(slot_sm, tile, lambda t, k, slot: row_copy(tile, buf, t, k, slot).start(priority=k % 2))

    @pl.when(i == 0)
    def _():
        first = _slot_copy(slot_hbm, slot_sm, ssem, 0)
        first.start()
        first.wait()
        start_gather(0, buf_a)
        _slot_copy(slot_hbm, slot_sm, ssem, 1).start()

    @pl.when(i + 2 < n)
    def _():
        _slot_copy(slot_hbm, slot_sm, ssem, i + 2).start()

    def work(buf, other):
        @pl.when(i + 1 < n)
        def _():
            _slot_copy(slot_hbm, slot_sm, ssem, i + 1).wait()
            start_gather(i + 1, other)

        _for_each_assignment(slot_sm, i, lambda t, k, slot: row_copy(i, buf, t, k, slot).wait())
        tw = tw_ref[...]
        parts = []
        for j in range(ROW_TILES):
            acc = jnp.zeros((TM_TOK, LANES), F32)
            for k in range(TOP_K):
                acc = acc + tw[:, k:k + 1] * buf[_lane_tile(j, TM_TOK, first_row=k * TM_TOK), :]
            parts.append(acc)
        y = _rms(x2_ref[...] + jnp.concatenate(parts, axis=-1)) * nf_ref[...]

        @pl.when(i < n - 1)
        def _():
            yp_ref[...] = y

        @pl.when(i == n - 1)
        def _():
            ys_ref[...] = y

    @pl.when(i % 2 == 0)
    def _():
        work(buf_a, buf_b)

    @pl.when(i % 2 == 1)
    def _():
        work(buf_b, buf_a)


def _combine(slot_flat, ys, x2, tw, norm_final):
    n_tiles = x2.shape[0] // TM_TOK
    n_prompt_tiles = n_tiles - 1
    return pl.pallas_call(
        _combine_kernel,
        grid=(n_tiles,),
        in_specs=[pl.BlockSpec(memory_space=pl.ANY), pl.BlockSpec(memory_space=pl.ANY),
                  pl.BlockSpec((TM_TOK, D_MODEL), lambda i: (i, 0)),
                  pl.BlockSpec((TM_TOK, LANES), lambda i: (i, 0)),
                  pl.BlockSpec((1, D_MODEL), lambda i: (0, 0))],
        out_specs=(pl.BlockSpec((TM_TOK, D_MODEL), lambda i: (jnp.minimum(i, n_prompt_tiles - 1), 0)),
                   pl.BlockSpec((TM_TOK, D_MODEL), lambda i: (0, 0))),
        out_shape=(jax.ShapeDtypeStruct((n_prompt_tiles * TM_TOK, D_MODEL), F32),
                   jax.ShapeDtypeStruct((TM_TOK, D_MODEL), F32)),
        scratch_shapes=[pltpu.SMEM((SLOT_RING * N_ASSIGN,), I32),
                        pltpu.VMEM((TOP_K * TM_TOK * ROW_TILES, LANES), F32),
                        pltpu.VMEM((TOP_K * TM_TOK * ROW_TILES, LANES), F32),
                        pltpu.SemaphoreType.DMA((2,)), pltpu.SemaphoreType.DMA((SLOT_RING,))],
        compiler_params=_cparams(("arbitrary",)),
        name="moe_combine",
    )(slot_flat, ys, x2, tw, norm_final)


def _head_pairs(c_bht):
    b, h, t = c_bht.shape
    return c_bht.reshape(b, N_HEAD_BLOCKS, HEADS_PER_BLOCK, t)


def _bias_lanes(pieces_bht):
    b, h, t = pieces_bht[0].shape
    x = jnp.stack(pieces_bht, axis=-1).astype(BF16)
    x = x.reshape(b, N_HEAD_BLOCKS, HEADS_PER_BLOCK, t, BIAS_PIECES).transpose(0, 1, 3, 2, 4)
    x = x.reshape(b, N_HEAD_BLOCKS, t, HEADS_PER_BLOCK * BIAS_PIECES)
    return jnp.pad(x, ((0, 0), (0, 0), (0, 0), (0, LANES - HEADS_PER_BLOCK * BIAS_PIECES)))


def _routing_tables(tidx, rank, counts, n_tiles_moe):
    padded = ((counts + TM_MOE - 1) // TM_MOE) * TM_MOE
    ends = jnp.cumsum(padded)
    offs = ends - padded
    experts = jnp.arange(N_EXPERTS, dtype=I32)
    off_of = jnp.sum(jnp.where(tidx[..., None] == experts, offs, 0), axis=-1)
    slot = (off_of + rank).reshape(-1).astype(I32)
    n_used = (ends[-1] // TM_MOE).astype(I32)
    tile_start = jnp.arange(n_tiles_moe, dtype=I32) * TM_MOE
    te = jnp.sum((tile_start[:, None] >= ends[None, :]).astype(I32), axis=-1)
    te_last = jnp.sum(((n_used - 1) * TM_MOE >= ends).astype(I32))
    te = jnp.minimum(jnp.where(jnp.arange(n_tiles_moe) < n_used, te, te_last), N_EXPERTS - 1).astype(I32)
    return slot, te, n_used.reshape(1), (offs + counts).astype(I32), (padded - counts).astype(I32)


def kernel(x_prompt, x_sample, cache_k, cache_v, cache_logf, norm_mix, w_in, b_f, norm_av, w_s, b_s,
           norm_out_a, norm_out_b, w_out, norm_ffn, w_router, b_router, w1, b1, w2, b2, norm_final):
    depth = w_in.shape[0]
    bsz, seq, _ = x_prompt.shape
    dbsz, dseq, _ = x_sample.shape
    past = cache_k.shape[2]
    n_prompt = bsz * seq
    n_sample = dbsz * dseq
    assert depth == 1, "the combine kernel applies the final norm, so it must follow the only layer"
    assert n_prompt % TM_PROJ == 0 and n_prompt % TM_TOK == 0 and seq % TQ == 0
    assert n_sample == MLP_CHUNK and MLP_CHUNK % dseq == 0 and n_sample <= TM_TOK
    n_pad = n_prompt + TM_TOK
    n_tiles_moe = (n_pad * TOP_K) // TM_MOE + N_EXPERTS
    n_slots = n_tiles_moe * TM_MOE

    xp = x_prompt.reshape(n_prompt, D_MODEL)
    xs = x_sample.reshape(n_sample, D_MODEL)
    outs = {k: [] for k in ("kp", "vp", "lfp", "ks", "vs", "lfs", "gvs")}
    for l in range(depth):
        win_b = jnp.pad(w_in[l], ((0, 0), (0, W_IN_COLS - w_in.shape[2]))).astype(BF16)
        bf_pad = jnp.pad(b_f[l], (0, LANES - N_B_HEADS)).reshape(1, LANES)
        nm = norm_mix[l].reshape(1, D_MODEL)
        nav = norm_av[l].reshape(1, D_A)
        reps = MLP_CHUNK // dseq
        ws_s = jnp.tile(w_s[l][:, :dseq, :dseq], (1, reps, reps))
        bst_p = b_s[l].T
        bst_s = jnp.tile(b_s[l][:, :dseq].T, (reps, 1))

        oa_p, q_p, k_p, v_p, kb_p, vb_p, lf_p, _ = _in_projection(
            xp, nm, win_b, bf_pad, nav, w_s[l], bst_p, tm=TM_PROJ, period=MLP_CHUNK)
        lf_p3 = lf_p.reshape(bsz, seq, N_B_HEADS)
        _, *neg_c_pieces = _cumsum_time(lf_p3.transpose(0, 2, 1))
        ob_p = _attention_prompt(q_p.reshape(bsz, seq, D_B), kb_p.reshape(bsz, seq, D_B),
                                 vb_p.reshape(bsz, seq, D_B).transpose(0, 2, 1), _bias_lanes(neg_c_pieces))

        oa_s, q_s, k_s, v_s, kb_s, vb_s, lf_s, vn_s = _in_projection(
            xs, nm, win_b, bf_pad, nav, ws_s, bst_s, tm=n_sample, period=dseq)
        lf_s3 = lf_s.reshape(dbsz, dseq, N_B_HEADS)
        t_all = past + dseq
        t_padded = -(-t_all // CS_BLK) * CS_BLK
        lf_all = jnp.concatenate([cache_logf[l].astype(F32), lf_s3], axis=1).transpose(0, 2, 1)
        c_all = _cumsum_time(jnp.pad(lf_all, ((0, 0), (0, 0), (0, t_padded - t_all))))[0]
        ob_s = _attention_sample(q_s.reshape(dbsz, dseq, D_B),
                                 cache_k[l].reshape(dbsz, past, D_B), cache_v[l].reshape(dbsz, past, D_B),
                                 kb_s.reshape(dbsz, dseq, D_B), vb_s.reshape(dbsz, dseq, D_B),
                                 _head_pairs(c_all[:, :, :past]), _head_pairs(c_all[:, :, past:t_all]))

        pad_rows = lambda a: jnp.pad(a, ((0, TM_TOK - n_sample), (0, 0)))
        wr_b = jnp.pad(w_router[l], ((0, 0), (0, LANES - N_EXPERTS))).astype(BF16)
        br_pad = jnp.pad(b_router[l], (0, LANES - N_EXPERTS)).reshape(1, LANES)
        x2, h3, tidx, tw, rank, cnt = _tail(
            oa_p, ob_p.reshape(n_prompt, D_B), xp, pad_rows(oa_s), pad_rows(ob_s.reshape(n_sample, D_B)),
            pad_rows(xs), norm_out_a[l].reshape(1, D_A), norm_out_b[l].reshape(1, D_B), w_out[l].astype(BF16),
            norm_ffn[l].reshape(1, D_MODEL), wr_b, br_pad)
        counts = cnt[0, :N_EXPERTS].astype(I32)
        slot, te, n_used, pad_start, pad_len = _routing_tables(tidx[:, :TOP_K], rank[:, :TOP_K], counts, n_tiles_moe)
        xsort = _dispatch(pad_start, pad_len, n_used, slot, h3, n_slots)
        ysort = _moe(te, n_used, xsort, w1[l], b1[l].reshape(N_EXPERTS, 1, 2 * D_FF), w2[l],
                     b2[l].reshape(N_EXPERTS, 1, D_MODEL))
        xp, y_tail = _combine(slot, ysort, x2, tw, norm_final.reshape(1, D_MODEL))
        xs = y_tail[:n_sample]

        outs["kp"].append(k_p.reshape(bsz, seq, N_B_HEADS, HEAD_DIM))
        outs["vp"].append(v_p.reshape(bsz, seq, N_B_HEADS, HEAD_DIM))
        outs["lfp"].append(lf_p3)
        outs["ks"].append(k_s.reshape(dbsz, dseq, N_B_HEADS, HEAD_DIM))
        outs["vs"].append(v_s.reshape(dbsz, dseq, N_B_HEADS, HEAD_DIM))
        outs["lfs"].append(lf_s3)
        outs["gvs"].append(vn_s.reshape(dbsz, dseq, D_A))

    y_prompt = xp.reshape(bsz, seq, D_MODEL)
    y_sample = xs.reshape(dbsz, dseq, D_MODEL)
    st = jnp.stack
    return (y_prompt, y_sample, st(outs["kp"]), st(outs["vp"]), st(outs["lfp"]),
            st(outs["ks"]), st(outs["vs"]), st(outs["lfs"]), st(outs["gvs"]))
```

```python
import functools

import jax
import jax.numpy as jnp
import numpy as np
from jax import lax
from jax.experimental import pallas as pl
from jax.experimental.pallas import tpu as pltpu

F32 = jnp.float32
BF16 = jnp.bfloat16
I32 = jnp.int32

D_MODEL = 1024
CHUNK = 64
MLP_CHUNK = 128
D_A = 512
A_GROUPS = 4
A_GROUP_CH = D_A // A_GROUPS
D_B = 512
N_B_HEADS = 8
HEAD_DIM = D_B // N_B_HEADS
N_EXPERTS = 32
TOP_K = 4
D_FF = D_MODEL
SWIGLU_ALPHA = 1.702
SWIGLU_LIMIT = 7.0
RMS_EPS = 1e-6

LANES = 128
SUBLANES = 8
ROW_TILES = D_MODEL // LANES
HEADS_PER_BLOCK = LANES // HEAD_DIM
N_HEAD_BLOCKS = N_B_HEADS // HEADS_PER_BLOCK
QKV_OFF = 2 * D_A
F_OFF = 2 * D_A + 3 * D_B
W_IN_COLS = F_OFF + LANES

TM_PROJ = 512
TM_TOK = 256
TM_MOE = 256
TQ = 256
TKB = 1024
KCH = 512
BIAS_PIECES = 3
CS_BLK = 256
FF_BLK = 512
VMEM_LIMIT = 48 * 1024 * 1024


def _cparams(sem, vmem=VMEM_LIMIT):
    return pltpu.CompilerParams(dimension_semantics=sem, vmem_limit_bytes=vmem)


def _gelu(x):
    return 0.5 * x * (1.0 + lax.erf(x * 0.7071067811865476))


def _log_sigmoid(x):
    return jnp.minimum(x, 0.0) - jnp.log1p(jnp.exp(-jnp.abs(x)))


def _sigmoid(x):
    return 1.0 / (1.0 + jnp.exp(-x))


def _rms(x):
    return x * lax.rsqrt(jnp.mean(x * x, axis=-1, keepdims=True) + RMS_EPS)


def _tile_rows(first_row, n_rows):
    return pl.ds(pl.multiple_of(first_row * ROW_TILES, ROW_TILES), n_rows * ROW_TILES)


def _lane_tile(j, n_rows, first_row=0):
    return pl.ds(first_row * ROW_TILES + j, n_rows, stride=ROW_TILES)


def _div_pow2(x, d):
    assert d & (d - 1) == 0
    return jnp.right_shift(x, d.bit_length() - 1)


def _mod_pow2(x, d):
    assert d & (d - 1) == 0
    return jnp.bitwise_and(x, d - 1)


def _inproj_kernel(x_ref, nm_ref, win_ref, bf_ref, nav_ref, ws_ref, bst_ref,
                   outa_ref, q_ref, k_ref, v_ref, kb_ref, vb_ref, logf_ref, vn_ref, *, period, v_transposed):
    tm = x_ref.shape[0]
    hn = (_rms(x_ref[...]) * nm_ref[...]).astype(BF16)

    def proj(lo, width):
        return jnp.dot(hn, win_ref[:, lo:lo + width], preferred_element_type=F32)

    u = _gelu(proj(0, D_A))
    vn = _rms(_gelu(proj(D_A, D_A))) * nav_ref[...]
    vn_ref[...] = vn
    q_ref[...] = (proj(QKV_OFF, D_B) * (HEAD_DIM ** -0.5)).astype(BF16)
    kk = proj(QKV_OFF + D_B, D_B)
    k_ref[...] = kk
    kb_ref[...] = kk.astype(BF16)
    vv = proj(QKV_OFF + 2 * D_B, D_B)
    v_ref[...] = vv
    if v_transposed:
        vb_ref[0] = vv.T.astype(BF16)
    else:
        vb_ref[...] = vv.astype(BF16)
    fl = proj(F_OFF, LANES) + bf_ref[...]
    logf_ref[...] = _log_sigmoid(fl)[:, :N_B_HEADS]

    vnb = vn.astype(BF16)
    ri = lax.broadcasted_iota(I32, (MLP_CHUNK, MLP_CHUNK), 0)
    ci = lax.broadcasted_iota(I32, (MLP_CHUNK, MLP_CHUNK), 1)
    same_stream = _div_pow2(ri, period) == _div_pow2(ci, period)
    visible = _div_pow2(_mod_pow2(ci, period), CHUNK) <= _div_pow2(_mod_pow2(ri, period), CHUNK)
    mask = jnp.logical_and(same_stream, visible)
    for g in range(A_GROUPS):
        w = jnp.where(mask, ws_ref[g], 0.0).astype(BF16)
        bcol = bst_ref[:, g:g + 1]
        cols = slice(g * A_GROUP_CH, (g + 1) * A_GROUP_CH)
        for c in range(tm // MLP_CHUNK):
            rows = slice(c * MLP_CHUNK, (c + 1) * MLP_CHUNK)
            mixed = jnp.dot(w, vnb[rows, cols], preferred_element_type=F32) + bcol
            outa_ref[rows, cols] = u[rows, cols] * mixed


def _in_projection(x2d, norm_mix, win_b, bf_pad, norm_av, ws_tiled, bst_tiled, *, tm, period, seq_for_vt=None):
    rows = x2d.shape[0]
    row_blk = lambda w: pl.BlockSpec((tm, w), lambda i: (i, 0))
    full = lambda shape: pl.BlockSpec(shape, lambda i: (0,) * len(shape))
    if seq_for_vt is None:
        vb_shape, vb_blk = jax.ShapeDtypeStruct((rows, D_B), BF16), row_blk(D_B)
    else:
        tiles = seq_for_vt // tm
        vb_shape = jax.ShapeDtypeStruct((rows // seq_for_vt, D_B, seq_for_vt), BF16)
        vb_blk = pl.BlockSpec((1, D_B, tm), lambda i: (i // tiles, 0, i % tiles))
    out_shape = (
        jax.ShapeDtypeStruct((rows, D_A), F32),
        jax.ShapeDtypeStruct((rows, D_B), BF16),
        jax.ShapeDtypeStruct((rows, D_B), F32),
        jax.ShapeDtypeStruct((rows, D_B), F32),
        jax.ShapeDtypeStruct((rows, D_B), BF16),
        vb_shape,
        jax.ShapeDtypeStruct((rows, N_B_HEADS), F32),
        jax.ShapeDtypeStruct((rows, D_A), F32),
    )
    return pl.pallas_call(
        functools.partial(_inproj_kernel, period=period, v_transposed=seq_for_vt is not None),
        grid=(rows // tm,),
        in_specs=[row_blk(D_MODEL), full((1, D_MODEL)), full((D_MODEL, W_IN_COLS)), full((1, LANES)),
                  full((1, D_A)), full((A_GROUPS, MLP_CHUNK, MLP_CHUNK)), full((MLP_CHUNK, A_GROUPS))],
        out_specs=(row_blk(D_A), row_blk(D_B), row_blk(D_B), row_blk(D_B), row_blk(D_B), vb_blk,
                   row_blk(N_B_HEADS), row_blk(D_A)),
        out_shape=out_shape,
        compiler_params=_cparams(("arbitrary",)),
        name="in_projection",
    )(x2d, norm_mix, win_b, bf_pad, norm_av, ws_tiled, bst_tiled)


def _split_bf16(x):
    x1 = x.astype(BF16)
    r1 = x - x1.astype(F32)
    x2 = r1.astype(BF16)
    x3 = (r1 - x2.astype(F32)).astype(BF16)
    return x1, x2, x3


def _cumsum_kernel(x_ref, o_ref, *piece_refs):
    t = x_ref.shape[2]
    ri = lax.broadcasted_iota(I32, (CS_BLK, CS_BLK), 0)
    ci = lax.broadcasted_iota(I32, (CS_BLK, CS_BLK), 1)
    upper = jnp.where(ri <= ci, 1.0, 0.0).astype(BF16)
    carry = jnp.zeros((N_B_HEADS, 1), F32)
    for c in range(t // CS_BLK):
        cols = slice(c * CS_BLK, (c + 1) * CS_BLK)
        s = sum(jnp.dot(piece, upper, preferred_element_type=F32) for piece in _split_bf16(x_ref[0, :, cols])) + carry
        o_ref[0, :, cols] = s
        for ref, piece in zip(piece_refs, _split_bf16(-s)):
            ref[0, :, cols] = piece.astype(F32)
        carry = s[:, CS_BLK - 1:CS_BLK]


def _cumsum_time(x_bht):
    b, h, t = x_bht.shape
    blk = pl.BlockSpec((1, h, t), lambda i: (i, 0, 0))
    return pl.pallas_call(
        _cumsum_kernel,
        grid=(b,),
        in_specs=[blk],
        out_specs=(blk,) * (1 + BIAS_PIECES),
        out_shape=(jax.ShapeDtypeStruct((b, h, t), F32),) * (1 + BIAS_PIECES),
        compiler_params=_cparams(("arbitrary",)),
        name="cumsum_logf",
    )(x_bht)


def _head_select(width=LANES):
    lane = lax.broadcasted_iota(I32, (1, width), 1)
    return [jnp.logical_and(lane >= h * HEAD_DIM, lane < (h + 1) * HEAD_DIM) for h in range(HEADS_PER_BLOCK)]


def _scores(qh, k2):
    return lax.dot_general(qh, k2, (((1,), (1,)), ((), ())), preferred_element_type=F32)


def _attn_prompt_kernel(q_ref, k_ref, vt_ref, cx_ref, o_ref, m_sc, l_sc, acc_sc, sa_sc, sb_sc):
    i = pl.program_id(2)
    lane = lax.broadcasted_iota(I32, (1, LANES), 1)
    q2 = q_ref[0]
    qaug = []
    for h in range(HEADS_PER_BLOCK):
        head = jnp.logical_and(lane >= h * HEAD_DIM, lane < (h + 1) * HEAD_DIM)
        pick = jnp.logical_and(lane >= h * BIAS_PIECES, lane < (h + 1) * BIAS_PIECES)
        head_row = jnp.where(head, 1.0, 0.0).astype(BF16)
        pick_row = jnp.where(pick, 1.0, 0.0).astype(BF16)
        qaug.append(jnp.concatenate([q2 * head_row, jnp.broadcast_to(pick_row, q2.shape)], axis=1))

    m_sc[...] = jnp.full(m_sc.shape, -jnp.inf, F32)
    l_sc[...] = jnp.zeros(l_sc.shape, F32)
    acc_sc[...] = jnp.zeros(acc_sc.shape, F32)

    def scores_into(c, s_sc):
        kc = pl.multiple_of(c * TQ, TQ)
        kk = jnp.concatenate([k_ref[0, pl.ds(kc, TQ), :], cx_ref[0, 0, pl.ds(kc, TQ), :]], axis=1)
        for h in range(HEADS_PER_BLOCK):
            s_sc[h] = lax.dot_general(kk, qaug[h], (((1,), (1,)), ((), ())), preferred_element_type=F32)

    def consume(c, s_sc, masked):
        kc = pl.multiple_of(c * TQ, TQ)
        for h in range(HEADS_PER_BLOCK):
            st = s_sc[h]
            if masked:
                key = lax.broadcasted_iota(I32, (TQ, TQ), 0)
                qry = lax.broadcasted_iota(I32, (TQ, TQ), 1)
                st = jnp.where(key <= qry, st, -jnp.inf)
            m_prev = m_sc[h]
            m_new = jnp.maximum(m_prev, jnp.max(st, axis=0, keepdims=True))
            a = jnp.exp(m_prev - m_new)
            p = jnp.exp(st - m_new)
            l_sc[h] = a * l_sc[h] + jnp.sum(p, axis=0, keepdims=True)
            m_sc[h] = m_new
            vt = vt_ref[0, h * HEAD_DIM:(h + 1) * HEAD_DIM, pl.ds(kc, TQ)]
            acc_sc[h] = a * acc_sc[h] + jnp.dot(vt, p.astype(BF16), preferred_element_type=F32)

    scores_into(0, sa_sc)

    def pair(j, carry):
        scores_into(2 * j + 1, sb_sc)
        consume(2 * j, sa_sc, False)
        scores_into(2 * j + 2, sa_sc)
        consume(2 * j + 1, sb_sc, False)
        return carry

    lax.fori_loop(0, i // 2, pair, 0)

    @pl.when(i % 2 == 0)
    def _():
        consume(i, sa_sc, True)

    @pl.when(i % 2 == 1)
    def _():
        scores_into(i, sb_sc)
        consume(i - 1, sa_sc, False)
        consume(i, sb_sc, True)

    out_t = jnp.concatenate([acc_sc[h] / l_sc[h] for h in range(HEADS_PER_BLOCK)], axis=0)
    o_ref[0] = out_t.T


def _attention_prompt(q, kb, vt, cx):
    b, s, _ = q.shape
    nq = s // TQ
    return pl.pallas_call(
        _attn_prompt_kernel,
        grid=(b, N_HEAD_BLOCKS, nq),
        in_specs=[pl.BlockSpec((1, TQ, LANES), lambda bi, hp, i: (bi, i, hp)),
                  pl.BlockSpec((1, s, LANES), lambda bi, hp, i: (bi, 0, hp)),
                  pl.BlockSpec((1, LANES, s), lambda bi, hp, i: (bi, hp, 0)),
                  pl.BlockSpec((1, 1, s, LANES), lambda bi, hp, i: (bi, hp, 0, 0))],
        out_specs=pl.BlockSpec((1, TQ, LANES), lambda bi, hp, i: (bi, i, hp)),
        out_shape=jax.ShapeDtypeStruct((b, s, D_B), F32),
        scratch_shapes=[pltpu.VMEM((HEADS_PER_BLOCK, 1, TQ), F32),
                        pltpu.VMEM((HEADS_PER_BLOCK, 1, TQ), F32),
                        pltpu.VMEM((HEADS_PER_BLOCK, HEAD_DIM, TQ), F32),
                        pltpu.VMEM((HEADS_PER_BLOCK, TQ, TQ), F32),
                        pltpu.VMEM((HEADS_PER_BLOCK, TQ, TQ), F32)],
        compiler_params=_cparams(("arbitrary", "arbitrary", "arbitrary")),
        name="fox_attention_prompt",
    )(q, kb, vt, cx)


def _attn_sample_kernel(q_ref, kc_ref, vc_ref, kn_ref, vn_ref, cc_ref, cn_ref, o_ref):
    t = q_ref.shape[1]
    sel = _head_select()
    q2 = q_ref[0]
    zero = jnp.zeros_like(q2)
    kc = kc_ref[0].astype(BF16)
    vc = vc_ref[0].astype(BF16)
    kn = kn_ref[0]
    vn = vn_ref[0]
    row = lax.broadcasted_iota(I32, (t, t), 0)
    col = lax.broadcasted_iota(I32, (t, t), 1)
    outs = []
    for h in range(HEADS_PER_BLOCK):
        qh = jnp.where(sel[h], q2, zero)
        cbase = cn_ref[0, 0, h:h + 1, 0:1]
        s_c = _scores(qh, kc) + (cbase - cc_ref[0, 0, h:h + 1, :])
        s_n = _scores(qh, kn) + (cbase - cn_ref[0, 0, h:h + 1, :])
        s_n = jnp.where(col <= row, s_n, -jnp.inf)
        m = jnp.maximum(jnp.max(s_c, axis=-1, keepdims=True), jnp.max(s_n, axis=-1, keepdims=True))
        p_c = jnp.exp(s_c - m)
        p_n = jnp.exp(s_n - m)
        l = jnp.sum(p_c, axis=-1, keepdims=True) + jnp.sum(p_n, axis=-1, keepdims=True)
        o = (jnp.dot(p_c.astype(BF16), vc, preferred_element_type=F32)
             + jnp.dot(p_n.astype(BF16), vn, preferred_element_type=F32))
        outs.append(o / l)
    o_ref[0] = jnp.where(sel[0], outs[0], outs[1])


def _attention_sample(q, cache_k, cache_v, kb, vb, c_cache, c_new):
    b, t, _ = q.shape
    p = cache_k.shape[1]
    newblk = pl.BlockSpec((1, t, LANES), lambda bi, hp: (bi, 0, hp))
    cacheblk = pl.BlockSpec((1, p, LANES), lambda bi, hp: (bi, 0, hp))
    return pl.pallas_call(
        _attn_sample_kernel,
        grid=(b, N_HEAD_BLOCKS),
        in_specs=[newblk, cacheblk, cacheblk, newblk, newblk,
                  pl.BlockSpec((1, 1, HEADS_PER_BLOCK, p), lambda bi, hp: (bi, hp, 0, 0)),
                  pl.BlockSpec((1, 1, HEADS_PER_BLOCK, t), lambda bi, hp: (bi, hp, 0, 0))],
        out_specs=newblk,
        out_shape=jax.ShapeDtypeStruct((b, t, D_B), F32),
        compiler_params=_cparams(("arbitrary", "arbitrary")),
        name="fox_attention_sample",
    )(q, cache_k, cache_v, kb, vb, c_cache, c_new)


def _tail_kernel(oap_ref, obp_ref, xp_ref, oas_ref, obs_ref, xs_ref, noa_ref, nob_ref, wout_ref,
                 nffn_ref, wr_ref, br_ref,
                 x2_ref, h3_ref, tidx_ref, tw_ref, rank_ref, cnt_ref, carry_sc):
    i = pl.program_id(0)
    is_sample = i == pl.num_programs(0) - 1
    oa = jnp.where(is_sample, oas_ref[...], oap_ref[...])
    ob = jnp.where(is_sample, obs_ref[...], obp_ref[...])
    x = jnp.where(is_sample, xs_ref[...], xp_ref[...])

    ma = (_rms(oa) * noa_ref[...]).astype(BF16)
    mb = (_rms(ob) * nob_ref[...]).astype(BF16)
    x2 = x + (jnp.dot(ma, wout_ref[:D_A, :], preferred_element_type=F32)
              + jnp.dot(mb, wout_ref[D_A:, :], preferred_element_type=F32))
    x2_ref[...] = x2
    h = _rms(x2) * nffn_ref[...]
    for j in range(ROW_TILES):
        h3_ref[_lane_tile(j, TM_TOK), :] = h[:, j * LANES:(j + 1) * LANES]

    lane = lax.broadcasted_iota(I32, (TM_TOK, LANES), 1)
    logits = jnp.dot(h.astype(BF16), wr_ref[...], preferred_element_type=F32) + br_ref[...]
    logits = jnp.where(lane < N_EXPERTS, logits, -jnp.inf)

    lane_f = lane.astype(F32)
    vals, idxs, hots = [], [], []
    work = logits
    for _ in range(TOP_K):
        mx = jnp.max(work, axis=-1, keepdims=True)
        idx = jnp.min(jnp.where(work == mx, lane_f, float(LANES)), axis=-1, keepdims=True)
        hot = lane_f == idx
        vals.append(mx)
        idxs.append(idx.astype(I32))
        hots.append(hot)
        work = jnp.where(hot, -jnp.inf, work)
    exps = [jnp.exp(v - vals[0]) for v in vals]
    denom = exps[0] + exps[1] + exps[2] + exps[3]

    @pl.when(i == 0)
    def _():
        carry_sc[...] = jnp.zeros(carry_sc.shape, F32)

    assigned = jnp.zeros((TM_TOK, LANES), F32)
    for hot in hots:
        assigned = assigned + jnp.where(hot, 1.0, 0.0)
    ri = lax.broadcasted_iota(I32, (TM_TOK, TM_TOK), 0)
    ci = lax.broadcasted_iota(I32, (TM_TOK, TM_TOK), 1)
    earlier = jnp.where(ci < ri, 1.0, 0.0).astype(BF16)
    before = jnp.dot(earlier, assigned.astype(BF16), preferred_element_type=F32) + carry_sc[...]

    tidx = jnp.zeros((TM_TOK, LANES), I32)
    tw = jnp.zeros((TM_TOK, LANES), F32)
    rank = jnp.zeros((TM_TOK, LANES), I32)
    for k in range(TOP_K):
        rk = jnp.sum(jnp.where(hots[k], before, 0.0), axis=-1, keepdims=True).astype(I32)
        tidx = jnp.where(lane == k, idxs[k], tidx)
        tw = jnp.where(lane == k, exps[k] / denom, tw)
        rank = jnp.where(lane == k, rk, rank)
    tidx_ref[...] = tidx
    tw_ref[...] = tw
    rank_ref[...] = rank
    total = carry_sc[...] + jnp.sum(assigned, axis=0, keepdims=True)
    carry_sc[...] = total
    cnt_ref[...] = jnp.broadcast_to(total, cnt_ref.shape)


def _tail(oa_p, ob_p, x_p, oa_s, ob_s, x_s, noa, nob, wout_b, nffn, wr_b, br_pad):
    n_prompt_tiles = x_p.shape[0] // TM_TOK
    n_tiles = n_prompt_tiles + 1
    n_pad = n_tiles * TM_TOK
    pblk = lambda w: pl.BlockSpec((TM_TOK, w), lambda i: (jnp.minimum(i, n_prompt_tiles - 1), 0))
    full = lambda shape: pl.BlockSpec(shape, lambda i: (0,) * len(shape))
    oblk = lambda w: pl.BlockSpec((TM_TOK, w), lambda i: (i, 0))
    out_shape = (
        jax.ShapeDtypeStruct((n_pad, D_MODEL), F32),
        jax.ShapeDtypeStruct((n_pad * ROW_TILES, LANES), F32),
        jax.ShapeDtypeStruct((n_pad, LANES), I32),
        jax.ShapeDtypeStruct((n_pad, LANES), F32),
        jax.ShapeDtypeStruct((n_pad, LANES), I32),
        jax.ShapeDtypeStruct((SUBLANES, LANES), F32),
    )
    return pl.pallas_call(
        _tail_kernel,
        grid=(n_tiles,),
        in_specs=[pblk(D_A), pblk(D_B), pblk(D_MODEL), full((TM_TOK, D_A)), full((TM_TOK, D_B)),
                  full((TM_TOK, D_MODEL)), full((1, D_A)), full((1, D_B)), full((D_MODEL, D_MODEL)),
                  full((1, D_MODEL)), full((D_MODEL, LANES)), full((1, LANES))],
        out_specs=(oblk(D_MODEL), pl.BlockSpec((TM_TOK * ROW_TILES, LANES), lambda i: (i, 0)),
                   oblk(LANES), oblk(LANES), oblk(LANES), full((SUBLANES, LANES))),
        out_shape=out_shape,
        scratch_shapes=[pltpu.VMEM((1, LANES), F32)],
        compiler_params=_cparams(("arbitrary",)),
        name="outproj_router",
    )(oa_p, ob_p, x_p, oa_s, ob_s, x_s, noa, nob, wout_b, nffn, wr_b, br_pad)


def _row_copy(src, src_row, dst, dst_row, sem):
    return pltpu.make_async_copy(src.at[_tile_rows(src_row, 1)], dst.at[_tile_rows(dst_row, 1)], sem)


N_ASSIGN = TM_TOK * TOP_K
SLOT_RING = 3
ISSUE_UNROLL = 4


def _slot_copy(slot_hbm, slot_sm, ssem, tile):
    place = tile % SLOT_RING
    return pltpu.make_async_copy(slot_hbm.at[pl.ds(pl.multiple_of(tile * N_ASSIGN, N_ASSIGN), N_ASSIGN)],
                                 slot_sm.at[pl.ds(pl.multiple_of(place * N_ASSIGN, N_ASSIGN), N_ASSIGN)],
                                 ssem.at[place])


def _for_each_assignment(slot_sm, tile, fn):
    base = (tile % SLOT_RING) * N_ASSIGN

    def body(g, carry):
        first = g * ISSUE_UNROLL
        slots = [slot_sm[base + first * TOP_K + a] for a in range(ISSUE_UNROLL * TOP_K)]
        for u in range(ISSUE_UNROLL):
            for k in range(TOP_K):
                fn(first + u, k, slots[u * TOP_K + k])
        return carry

    lax.fori_loop(0, TM_TOK // ISSUE_UNROLL, body, 0)


def _dispatch_kernel(pad_start_ref, pad_len_ref, nu_ref, slot_hbm, h3_hbm, xs_hbm,
                     slot_sm, stage, zblk, sem, ssem, hsem, zsem):
    i = pl.program_id(0)
    n = pl.num_programs(0)

    def stage_copy(tile):
        place = tile % SLOT_RING
        return pltpu.make_async_copy(h3_hbm.at[_tile_rows(tile * TM_TOK, TM_TOK)],
                                     stage.at[_tile_rows(place * TM_TOK, TM_TOK)], hsem.at[place])

    @pl.when(i == 0)
    def _():
        _slot_copy(slot_hbm, slot_sm, ssem, 0).start()
        stage_copy(0).start()

    @pl.when(i + 1 < n)
    def _():
        _slot_copy(slot_hbm, slot_sm, ssem, i + 1).start()
        stage_copy(i + 1).start()

    _slot_copy(slot_hbm, slot_sm, ssem, i).wait()
    stage_copy(i).wait()

    def row_copy(tile, t, slot):
        return _row_copy(stage, (tile % SLOT_RING) * TM_TOK + t, xs_hbm, slot, sem.at[tile % 2])

    _for_each_assignment(slot_sm, i, lambda t, k, slot: row_copy(i, t, slot).start(priority=k % 2))

    @pl.when(i > 0)
    def _():
        _for_each_assignment(slot_sm, i - 1, lambda t, k, slot: row_copy(i - 1, t, slot).wait())

    @pl.when(i == n - 1)
    def _():
        _for_each_assignment(slot_sm, i, lambda t, k, slot: row_copy(i, t, slot).wait())
        zblk[...] = jnp.zeros(zblk.shape, F32)
        for e in range(N_EXPERTS):
            start = pad_start_ref[e]
            n_pad_rows = pad_len_ref[e]

            def zissue(r, carry):
                _row_copy(zblk, 0, xs_hbm, start + r, zsem).start()
                return carry

            def zdrain(r, carry):
                _row_copy(zblk, 0, xs_hbm, start + r, zsem).wait()
                return carry

            lax.fori_loop(0, n_pad_rows, zissue, 0)
            lax.fori_loop(0, n_pad_rows, zdrain, 0)

        def tile_copy(r):
            return pltpu.make_async_copy(zblk, xs_hbm.at[_tile_rows(r * TM_MOE, TM_MOE)], zsem)

        def tissue(r, carry):
            tile_copy(r).start()
            return carry

        def tdrain(r, carry):
            tile_copy(r).wait()
            return carry

        n_tiles_moe = xs_hbm.shape[0] // (TM_MOE * ROW_TILES)
        lax.fori_loop(nu_ref[0], n_tiles_moe, tissue, 0)
        lax.fori_loop(nu_ref[0], n_tiles_moe, tdrain, 0)


def _dispatch(pad_start, pad_len, n_used, slot_flat, h3, n_slots):
    n_tiles = h3.shape[0] // (TM_TOK * ROW_TILES)
    grid_spec = pltpu.PrefetchScalarGridSpec(
        num_scalar_prefetch=3,
        grid=(n_tiles,),
        in_specs=[pl.BlockSpec(memory_space=pl.ANY), pl.BlockSpec(memory_space=pl.ANY)],
        out_specs=pl.BlockSpec(memory_space=pl.ANY),
        scratch_shapes=[pltpu.SMEM((SLOT_RING * N_ASSIGN,), I32),
                        pltpu.VMEM((SLOT_RING * TM_TOK * ROW_TILES, LANES), F32),
                        pltpu.VMEM((TM_MOE * ROW_TILES, LANES), F32),
                        pltpu.SemaphoreType.DMA((2,)), pltpu.SemaphoreType.DMA((SLOT_RING,)),
                        pltpu.SemaphoreType.DMA((SLOT_RING,)), pltpu.SemaphoreType.DMA],
    )
    return pl.pallas_call(
        _dispatch_kernel,
        grid_spec=grid_spec,
        out_shape=jax.ShapeDtypeStruct((n_slots * ROW_TILES, LANES), F32),
        compiler_params=_cparams(("arbitrary",)),
        name="moe_dispatch",
    )(pad_start, pad_len, n_used, slot_flat, h3)


def _moe_kernel(te_ref, nu_ref, xs_ref, w1_ref, b1_ref, w2_ref, b2_ref, ys_ref, w1b_sc, w2b_sc):
    i = pl.program_id(0)
    prev = te_ref[jnp.maximum(i - 1, 0)]
    new_expert = jnp.logical_or(i == 0, te_ref[i] != prev)

    @pl.when(new_expert)
    def _():
        blk = 128
        for r in range(D_MODEL // blk):
            w1b_sc[r * blk:(r + 1) * blk, :] = w1_ref[0, r * blk:(r + 1) * blk, :].astype(BF16)
        for r in range(D_FF // blk):
            w2b_sc[r * blk:(r + 1) * blk, :] = w2_ref[0, r * blk:(r + 1) * blk, :].astype(BF16)

    @pl.when(i < nu_ref[0])
    def _():
        xb = jnp.concatenate([xs_ref[_lane_tile(j, TM_MOE), :] for j in range(ROW_TILES)], axis=-1).astype(BF16)
        acc = jnp.zeros((TM_MOE, D_MODEL), F32)
        for c in range(D_FF // FF_BLK):
            lo = c * FF_BLK
            glu = jnp.dot(xb, w1b_sc[:, lo:lo + FF_BLK], preferred_element_type=F32) + b1_ref[0, :, lo:lo + FF_BLK]
            lin = (jnp.dot(xb, w1b_sc[:, D_FF + lo:D_FF + lo + FF_BLK], preferred_element_type=F32)
                   + b1_ref[0, :, D_FF + lo:D_FF + lo + FF_BLK])
            glu = jnp.minimum(glu, SWIGLU_LIMIT)
            lin = jnp.clip(lin, -SWIGLU_LIMIT, SWIGLU_LIMIT)
            act = glu * _sigmoid(SWIGLU_ALPHA * glu) * (lin + 1.0)
            acc = acc + jnp.dot(act.astype(BF16), w2b_sc[lo:lo + FF_BLK, :], preferred_element_type=F32)
        out = acc + b2_ref[0]
        for j in range(ROW_TILES):
            ys_ref[_lane_tile(j, TM_MOE), :] = out[:, j * LANES:(j + 1) * LANES]

    @pl.when(i >= nu_ref[0])
    def _():
        ys_ref[...] = jnp.zeros(ys_ref.shape, F32)


def _moe(tile_expert, n_used, xs, w1, b1, w2, b2):
    n_tiles = xs.shape[0] // (TM_MOE * ROW_TILES)
    exp_map = lambda i, te, nu: (te[i], 0, 0)
    grid_spec = pltpu.PrefetchScalarGridSpec(
        num_scalar_prefetch=2,
        grid=(n_tiles,),
        in_specs=[pl.BlockSpec((TM_MOE * ROW_TILES, LANES), lambda i, te, nu: (jnp.minimum(i, nu[0] - 1), 0)),
                  pl.BlockSpec((1, D_MODEL, 2 * D_FF), exp_map),
                  pl.BlockSpec((1, 1, 2 * D_FF), exp_map),
                  pl.BlockSpec((1, D_FF, D_MODEL), exp_map),
                  pl.BlockSpec((1, 1, D_MODEL), exp_map)],
        out_specs=pl.BlockSpec((TM_MOE * ROW_TILES, LANES), lambda i, te, nu: (i, 0)),
        scratch_shapes=[pltpu.VMEM((D_MODEL, 2 * D_FF), BF16), pltpu.VMEM((D_FF, D_MODEL), BF16)],
    )
    return pl.pallas_call(
        _moe_kernel,
        grid_spec=grid_spec,
        out_shape=jax.ShapeDtypeStruct(xs.shape, F32),
        compiler_params=_cparams(("arbitrary",), vmem=56 * 1024 * 1024),
        name="moe_experts",
    )(tile_expert, n_used, xs, w1, b1, w2, b2)


def _combine_kernel(slot_hbm, ys_hbm, x2_ref, tw_ref, nf_ref, yp_ref, ys_ref, slot_sm, buf_a, buf_b, sem, ssem):
    i = pl.program_id(0)
    n = pl.num_programs(0)

    def row_copy(tile, buf, t, k, slot):
        return _row_copy(ys_hbm, slot, buf, k * TM_TOK + t, sem.at[tile % 2])

    def start_gather(tile, buf):
        _for_each_assignment(slot_sm, tile, lambda t, k, slot: row_copy(tile, buf, t, k, slot).start(priority=k % 2))

    @pl.when(i == 0)
    def _():
        first = _slot_copy(slot_hbm, slot_sm, ssem, 0)
        first.start()
        first.wait()
        start_gather(0, buf_a)
        _slot_copy(slot_hbm, slot_sm, ssem, 1).start()

    @pl.when(i + 2 < n)
    def _():
        _slot_copy(slot_hbm, slot_sm, ssem, i + 2).start()

    def work(buf, other):
        @pl.when(i + 1 < n)
        def _():
            _slot_copy(slot_hbm, slot_sm, ssem, i + 1).wait()
            start_gather(i + 1, other)

        _for_each_assignment(slot_sm, i, lambda t, k, slot: row_copy(i, buf, t, k, slot).wait())
        tw = tw_ref[...]
        parts = []
        for j in range(ROW_TILES):
            acc = jnp.zeros((TM_TOK, LANES), F32)
            for k in range(TOP_K):
                acc = acc + tw[:, k:k + 1] * buf[_lane_tile(j, TM_TOK, first_row=k * TM_TOK), :]
            parts.append(acc)
        y = _rms(x2_ref[...] + jnp.concatenate(parts, axis=-1)) * nf_ref[...]

        @pl.when(i < n - 1)
        def _():
            yp_ref[...] = y

        @pl.when(i == n - 1)
        def _():
            ys_ref[...] = y

    @pl.when(i % 2 == 0)
    def _():
        work(buf_a, buf_b)

    @pl.when(i % 2 == 1)
    def _():
        work(buf_b, buf_a)


def _combine(slot_flat, ys, x2, tw, norm_final):
    n_tiles = x2.shape[0] // TM_TOK
    n_prompt_tiles = n_tiles - 1
    return pl.pallas_call(
        _combine_kernel,
        grid=(n_tiles,),
        in_specs=[pl.BlockSpec(memory_space=pl.ANY), pl.BlockSpec(memory_space=pl.ANY),
                  pl.BlockSpec((TM_TOK, D_MODEL), lambda i: (i, 0)),
                  pl.BlockSpec((TM_TOK, LANES), lambda i: (i, 0)),
                  pl.BlockSpec((1, D_MODEL), lambda i: (0, 0))],
        out_specs=(pl.BlockSpec((TM_TOK, D_MODEL), lambda i: (jnp.minimum(i, n_prompt_tiles - 1), 0)),
                   pl.BlockSpec((TM_TOK, D_MODEL), lambda i: (0, 0))),
        out_shape=(jax.ShapeDtypeStruct((n_prompt_tiles * TM_TOK, D_MODEL), F32),
                   jax.ShapeDtypeStruct((TM_TOK, D_MODEL), F32)),
        scratch_shapes=[pltpu.SMEM((SLOT_RING * N_ASSIGN,), I32),
                        pltpu.VMEM((TOP_K * TM_TOK * ROW_TILES, LANES), F32),
                        pltpu.VMEM((TOP_K * TM_TOK * ROW_TILES, LANES), F32),
                        pltpu.SemaphoreType.DMA((2,)), pltpu.SemaphoreType.DMA((SLOT_RING,))],
        compiler_params=_cparams(("arbitrary",)),
        name="moe_combine",
    )(slot_flat, ys, x2, tw, norm_final)


def _head_pairs(c_bht):
    b, h, t = c_bht.shape
    return c_bht.reshape(b, N_HEAD_BLOCKS, HEADS_PER_BLOCK, t)


def _bias_lanes(pieces_bht):
    b, h, t = pieces_bht[0].shape
    x = jnp.stack(pieces_bht, axis=-1).astype(BF16)
    x = x.reshape(b, N_HEAD_BLOCKS, HEADS_PER_BLOCK, t, BIAS_PIECES).transpose(0, 1, 3, 2, 4)
    x = x.reshape(b, N_HEAD_BLOCKS, t, HEADS_PER_BLOCK * BIAS_PIECES)
    return jnp.pad(x, ((0, 0), (0, 0), (0, 0), (0, LANES - HEADS_PER_BLOCK * BIAS_PIECES)))


def _routing_tables(tidx, rank, counts, n_tiles_moe):
    padded = ((counts + TM_MOE - 1) // TM_MOE) * TM_MOE
    ends = jnp.cumsum(padded)
    offs = ends - padded
    experts = jnp.arange(N_EXPERTS, dtype=I32)
    off_of = jnp.sum(jnp.where(tidx[..., None] == experts, offs, 0), axis=-1)
    slot = (off_of + rank).reshape(-1).astype(I32)
    n_used = (ends[-1] // TM_MOE).astype(I32)
    tile_start = jnp.arange(n_tiles_moe, dtype=I32) * TM_MOE
    te = jnp.sum((tile_start[:, None] >= ends[None, :]).astype(I32), axis=-1)
    te_last = jnp.sum(((n_used - 1) * TM_MOE >= ends).astype(I32))
    te = jnp.minimum(jnp.where(jnp.arange(n_tiles_moe) < n_used, te, te_last), N_EXPERTS - 1).astype(I32)
    return slot, te, n_used.reshape(1), (offs + counts).astype(I32), (padded - counts).astype(I32)


def kernel(x_prompt, x_sample, cache_k, cache_v, cache_logf, norm_mix, w_in, b_f, norm_av, w_s, b_s,
           norm_out_a, norm_out_b, w_out, norm_ffn, w_router, b_router, w1, b1, w2, b2, norm_final):
    depth = w_in.shape[0]
    bsz, seq, _ = x_prompt.shape
    dbsz, dseq, _ = x_sample.shape
    past = cache_k.shape[2]
    n_prompt = bsz * seq
    n_sample = dbsz * dseq
    assert depth == 1, "the combine kernel applies the final norm, so it must follow the only layer"
    assert seq % TM_PROJ == 0 and n_prompt % TM_TOK == 0 and seq % TQ == 0
    assert n_sample == MLP_CHUNK and MLP_CHUNK % dseq == 0 and n_sample <= TM_TOK
    n_pad = n_prompt + TM_TOK
    n_tiles_moe = (n_pad * TOP_K) // TM_MOE + N_EXPERTS
    n_slots = n_tiles_moe * TM_MOE

    xp = x_prompt.reshape(n_prompt, D_MODEL)
    xs = x_sample.reshape(n_sample, D_MODEL)
    outs = {k: [] for k in ("kp", "vp", "lfp", "ks", "vs", "lfs", "gvs")}
    for l in range(depth):
        win_b = jnp.pad(w_in[l], ((0, 0), (0, W_IN_COLS - w_in.shape[2]))).astype(BF16)
        bf_pad = jnp.pad(b_f[l], (0, LANES - N_B_HEADS)).reshape(1, LANES)
        nm = norm_mix[l].reshape(1, D_MODEL)
        nav = norm_av[l].reshape(1, D_A)
        reps = MLP_CHUNK // dseq
        ws_s = jnp.tile(w_s[l][:, :dseq, :dseq], (1, reps, reps))
        bst_p = b_s[l].T
        bst_s = jnp.tile(b_s[l][:, :dseq].T, (reps, 1))

        oa_p, q_p, k_p, v_p, kb_p, vt_p, lf_p, _ = _in_projection(
            xp, nm, win_b, bf_pad, nav, w_s[l], bst_p, tm=TM_PROJ, period=MLP_CHUNK, seq_for_vt=seq)
        lf_p3 = lf_p.reshape(bsz, seq, N_B_HEADS)
        _, *neg_c_pieces = _cumsum_time(lf_p3.transpose(0, 2, 1))
        ob_p = _attention_prompt(q_p.reshape(bsz, seq, D_B), kb_p.reshape(bsz, seq, D_B),
                                 vt_p, _bias_lanes(neg_c_pieces))

        oa_s, q_s, k_s, v_s, kb_s, vb_s, lf_s, vn_s = _in_projection(
            xs, nm, win_b, bf_pad, nav, ws_s, bst_s, tm=n_sample, period=dseq)
        lf_s3 = lf_s.reshape(dbsz, dseq, N_B_HEADS)
        t_all = past + dseq
        t_padded = -(-t_all // CS_BLK) * CS_BLK
        lf_all = jnp.concatenate([cache_logf[l].astype(F32), lf_s3], axis=1).transpose(0, 2, 1)
        c_all = _cumsum_time(jnp.pad(lf_all, ((0, 0), (0, 0), (0, t_padded - t_all))))[0]
        ob_s = _attention_sample(q_s.reshape(dbsz, dseq, D_B),
                                 cache_k[l].reshape(dbsz, past, D_B), cache_v[l].reshape(dbsz, past, D_B),
                                 kb_s.reshape(dbsz, dseq, D_B), vb_s.reshape(dbsz, dseq, D_B),
                                 _head_pairs(c_all[:, :, :past]), _head_pairs(c_all[:, :, past:t_all]))

        pad_rows = lambda a: jnp.pad(a, ((0, TM_TOK - n_sample), (0, 0)))
        wr_b = jnp.pad(w_router[l], ((0, 0), (0, LANES - N_EXPERTS))).astype(BF16)
        br_pad = jnp.pad(b_router[l], (0, LANES - N_EXPERTS)).reshape(1, LANES)
        x2, h3, tidx, tw, rank, cnt = _tail(
            oa_p, ob_p.reshape(n_prompt, D_B), xp, pad_rows(oa_s), pad_rows(ob_s.reshape(n_sample, D_B)),
            pad_rows(xs), norm_out_a[l].reshape(1, D_A), norm_out_b[l].reshape(1, D_B), w_out[l].astype(BF16),
            norm_ffn[l].reshape(1, D_MODEL), wr_b, br_pad)
        counts = cnt[0, :N_EXPERTS].astype(I32)
        slot, te, n_used, pad_start, pad_len = _routing_tables(tidx[:, :TOP_K], rank[:, :TOP_K], counts, n_tiles_moe)
        xsort = _dispatch(pad_start, pad_len, n_used, slot, h3, n_slots)
        ysort = _moe(te, n_used, xsort, w1[l], b1[l].reshape(N_EXPERTS, 1, 2 * D_FF), w2[l],
                     b2[l].reshape(N_EXPERTS, 1, D_MODEL))
        xp, y_tail = _combine(slot, ysort, x2, tw, norm_final.reshape(1, D_MODEL))
        xs = y_tail[:n_sample]

        outs["kp"].append(k_p.reshape(bsz, seq, N_B_HEADS, HEAD_DIM))
        outs["vp"].append(v_p.reshape(bsz, seq, N_B_HEADS, HEAD_DIM))
        outs["lfp"].append(lf_p3)
        outs["ks"].append(k_s.reshape(dbsz, dseq, N_B_HEADS, HEAD_DIM))
        outs["vs"].append(v_s.reshape(dbsz, dseq, N_B_HEADS, HEAD_DIM))
        outs["lfs"].append(lf_s3)
        outs["gvs"].append(vn_s.reshape(dbsz, dseq, D_A))

    y_prompt = xp.reshape(bsz, seq, D_MODEL)
    y_sample = xs.reshape(dbsz, dseq, D_MODEL)
    st = jnp.stack
    return (y_prompt, y_sample, st(outs["kp"]), st(outs["vp"]), st(outs["lfp"]),
            st(outs["ks"]), st(outs["vs"]), st(outs["lfs"]), st(outs["gvs"]))
```

```python
import functools

import jax
import jax.numpy as jnp
import numpy as np
from jax import lax
from jax.experimental import pallas as pl
from jax.experimental.pallas import tpu as pltpu

F32 = jnp.float32
BF16 = jnp.bfloat16
I32 = jnp.int32

D_MODEL = 1024
CHUNK = 64
MLP_CHUNK = 128
D_A = 512
A_GROUPS = 4
A_GROUP_CH = D_A // A_GROUPS
D_B = 512
N_B_HEADS = 8
HEAD_DIM = D_B // N_B_HEADS
N_EXPERTS = 32
TOP_K = 4
D_FF = D_MODEL
SWIGLU_ALPHA = 1.702
SWIGLU_LIMIT = 7.0
RMS_EPS = 1e-6

LANES = 128
SUBLANES = 8
ROW_TILES = D_MODEL // LANES
HEADS_PER_BLOCK = LANES // HEAD_DIM
N_HEAD_BLOCKS = N_B_HEADS // HEADS_PER_BLOCK
QKV_OFF = 2 * D_A
F_OFF = 2 * D_A + 3 * D_B
W_IN_COLS = F_OFF + LANES

TM_PROJ = 512
TM_TOK = 256
TM_MOE = 256
TQ = 256
TKB = 1024
KCH = 512
BIAS_PIECES = 3
CS_BLK = 256
FF_BLK = 512
VMEM_LIMIT = 48 * 1024 * 1024


def _cparams(sem, vmem=VMEM_LIMIT):
    return pltpu.CompilerParams(dimension_semantics=sem, vmem_limit_bytes=vmem)


def _gelu(x):
    return 0.5 * x * (1.0 + lax.erf(x * 0.7071067811865476))


def _log_sigmoid(x):
    return jnp.minimum(x, 0.0) - jnp.log1p(jnp.exp(-jnp.abs(x)))


def _sigmoid(x):
    return 1.0 / (1.0 + jnp.exp(-x))


def _rms(x):
    return x * lax.rsqrt(jnp.mean(x * x, axis=-1, keepdims=True) + RMS_EPS)


def _tile_rows(first_row, n_rows):
    return pl.ds(pl.multiple_of(first_row * ROW_TILES, ROW_TILES), n_rows * ROW_TILES)


def _lane_tile(j, n_rows, first_row=0):
    return pl.ds(first_row * ROW_TILES + j, n_rows, stride=ROW_TILES)


def _div_pow2(x, d):
    assert d & (d - 1) == 0
    return jnp.right_shift(x, d.bit_length() - 1)


def _mod_pow2(x, d):
    assert d & (d - 1) == 0
    return jnp.bitwise_and(x, d - 1)


def _store_heads(ref, x):
    ref[...] = x.reshape(x.shape[0], N_B_HEADS, HEAD_DIM)


def _inproj_kernel(x_ref, nm_ref, win_ref, bf_ref, nav_ref, ws_ref, bst_ref,
                   outa_ref, q_ref, k_ref, v_ref, kb_ref, vb_ref, logf_ref, vn_ref, *, period, v_transposed):
    tm = x_ref.shape[0]
    hn = (_rms(x_ref[...]) * nm_ref[...]).astype(BF16)

    def proj(lo, width):
        return jnp.dot(hn, win_ref[:, lo:lo + width], preferred_element_type=F32)

    u = _gelu(proj(0, D_A))
    vn = _rms(_gelu(proj(D_A, D_A))) * nav_ref[...]
    vn_ref[...] = vn
    q_ref[...] = (proj(QKV_OFF, D_B) * (HEAD_DIM ** -0.5)).astype(BF16)
    kk = proj(QKV_OFF + D_B, D_B)
    _store_heads(k_ref, kk)
    kb_ref[...] = kk.astype(BF16)
    vv = proj(QKV_OFF + 2 * D_B, D_B)
    _store_heads(v_ref, vv)
    if v_transposed:
        vb_ref[0] = vv.T.astype(BF16)
    else:
        vb_ref[...] = vv.astype(BF16)
    fl = proj(F_OFF, LANES) + bf_ref[...]
    logf_ref[...] = _log_sigmoid(fl)[:, :N_B_HEADS]

    vnb = vn.astype(BF16)
    ri = lax.broadcasted_iota(I32, (MLP_CHUNK, MLP_CHUNK), 0)
    ci = lax.broadcasted_iota(I32, (MLP_CHUNK, MLP_CHUNK), 1)
    same_stream = _div_pow2(ri, period) == _div_pow2(ci, period)
    visible = _div_pow2(_mod_pow2(ci, period), CHUNK) <= _div_pow2(_mod_pow2(ri, period), CHUNK)
    mask = jnp.logical_and(same_stream, visible)
    for g in range(A_GROUPS):
        w = jnp.where(mask, ws_ref[g], 0.0).astype(BF16)
        bcol = bst_ref[:, g:g + 1]
        cols = slice(g * A_GROUP_CH, (g + 1) * A_GROUP_CH)
        for c in range(tm // MLP_CHUNK):
            rows = slice(c * MLP_CHUNK, (c + 1) * MLP_CHUNK)
            mixed = jnp.dot(w, vnb[rows, cols], preferred_element_type=F32) + bcol
            outa_ref[rows, cols] = u[rows, cols] * mixed


def _in_projection(x2d, norm_mix, win_b, bf_pad, norm_av, ws_tiled, bst_tiled, *, tm, period, seq_for_vt=None):
    rows = x2d.shape[0]
    row_blk = lambda w: pl.BlockSpec((tm, w), lambda i: (i, 0))
    full = lambda shape: pl.BlockSpec(shape, lambda i: (0,) * len(shape))
    head_blk = pl.BlockSpec((tm, N_B_HEADS, HEAD_DIM), lambda i: (i, 0, 0))
    if seq_for_vt is None:
        vb_shape, vb_blk = jax.ShapeDtypeStruct((rows, D_B), BF16), row_blk(D_B)
    else:
        tiles = seq_for_vt // tm
        vb_shape = jax.ShapeDtypeStruct((rows // seq_for_vt, D_B, seq_for_vt), BF16)
        vb_blk = pl.BlockSpec((1, D_B, tm), lambda i: (i // tiles, 0, i % tiles))
    out_shape = (
        jax.ShapeDtypeStruct((rows, D_A), F32),
        jax.ShapeDtypeStruct((rows, D_B), BF16),
        jax.ShapeDtypeStruct((rows, N_B_HEADS, HEAD_DIM), F32),
        jax.ShapeDtypeStruct((rows, N_B_HEADS, HEAD_DIM), F32),
        jax.ShapeDtypeStruct((rows, D_B), BF16),
        vb_shape,
        jax.ShapeDtypeStruct((rows, N_B_HEADS), F32),
        jax.ShapeDtypeStruct((rows, D_A), F32),
    )
    return pl.pallas_call(
        functools.partial(_inproj_kernel, period=period, v_transposed=seq_for_vt is not None),
        grid=(rows // tm,),
        in_specs=[row_blk(D_MODEL), full((1, D_MODEL)), full((D_MODEL, W_IN_COLS)), full((1, LANES)),
                  full((1, D_A)), full((A_GROUPS, MLP_CHUNK, MLP_CHUNK)), full((MLP_CHUNK, A_GROUPS))],
        out_specs=(row_blk(D_A), row_blk(D_B), head_blk, head_blk, row_blk(D_B), vb_blk,
                   row_blk(N_B_HEADS), row_blk(D_A)),
        out_shape=out_shape,
        compiler_params=_cparams(("arbitrary",)),
        name="in_projection",
    )(x2d, norm_mix, win_b, bf_pad, norm_av, ws_tiled, bst_tiled)


def _split_bf16(x):
    x1 = x.astype(BF16)
    r1 = x - x1.astype(F32)
    x2 = r1.astype(BF16)
    x3 = (r1 - x2.astype(F32)).astype(BF16)
    return x1, x2, x3


def _cumsum_kernel(x_ref, o_ref, *piece_refs):
    t = x_ref.shape[2]
    ri = lax.broadcasted_iota(I32, (CS_BLK, CS_BLK), 0)
    ci = lax.broadcasted_iota(I32, (CS_BLK, CS_BLK), 1)
    upper = jnp.where(ri <= ci, 1.0, 0.0).astype(BF16)
    carry = jnp.zeros((N_B_HEADS, 1), F32)
    for c in range(t // CS_BLK):
        cols = slice(c * CS_BLK, (c + 1) * CS_BLK)
        s = sum(jnp.dot(piece, upper, preferred_element_type=F32) for piece in _split_bf16(x_ref[0, :, cols])) + carry
        o_ref[0, :, cols] = s
        for ref, piece in zip(piece_refs, _split_bf16(-s)):
            ref[0, :, cols] = piece.astype(F32)
        carry = s[:, CS_BLK - 1:CS_BLK]


def _cumsum_time(x_bht):
    b, h, t = x_bht.shape
    blk = pl.BlockSpec((1, h, t), lambda i: (i, 0, 0))
    return pl.pallas_call(
        _cumsum_kernel,
        grid=(b,),
        in_specs=[blk],
        out_specs=(blk,) * (1 + BIAS_PIECES),
        out_shape=(jax.ShapeDtypeStruct((b, h, t), F32),) * (1 + BIAS_PIECES),
        compiler_params=_cparams(("arbitrary",)),
        name="cumsum_logf",
    )(x_bht)


def _head_select(width=LANES):
    lane = lax.broadcasted_iota(I32, (1, width), 1)
    return [jnp.logical_and(lane >= h * HEAD_DIM, lane < (h + 1) * HEAD_DIM) for h in range(HEADS_PER_BLOCK)]


def _scores(qh, k2):
    return lax.dot_general(qh, k2, (((1,), (1,)), ((), ())), preferred_element_type=F32)


def _attn_prompt_kernel(q_ref, k_ref, vt_ref, cx_ref, o_ref, m_sc, l_sc, acc_sc, sa_sc, sb_sc):
    i = pl.program_id(2)
    lane = lax.broadcasted_iota(I32, (1, LANES), 1)
    q2 = q_ref[0]
    qaug = []
    for h in range(HEADS_PER_BLOCK):
        head = jnp.logical_and(lane >= h * HEAD_DIM, lane < (h + 1) * HEAD_DIM)
        pick = jnp.logical_and(lane >= h * BIAS_PIECES, lane < (h + 1) * BIAS_PIECES)
        head_row = jnp.where(head, 1.0, 0.0).astype(BF16)
        pick_row = jnp.where(pick, 1.0, 0.0).astype(BF16)
        qaug.append(jnp.concatenate([q2 * head_row, jnp.broadcast_to(pick_row, q2.shape)], axis=1))

    m_sc[...] = jnp.full(m_sc.shape, -jnp.inf, F32)
    l_sc[...] = jnp.zeros(l_sc.shape, F32)
    acc_sc[...] = jnp.zeros(acc_sc.shape, F32)

    def scores_into(c, s_sc):
        kc = pl.multiple_of(c * TQ, TQ)
        kk = jnp.concatenate([k_ref[0, pl.ds(kc, TQ), :], cx_ref[0, 0, pl.ds(kc, TQ), :]], axis=1)
        for h in range(HEADS_PER_BLOCK):
            s_sc[h] = lax.dot_general(kk, qaug[h], (((1,), (1,)), ((), ())), preferred_element_type=F32)

    def consume(c, s_sc, masked):
        kc = pl.multiple_of(c * TQ, TQ)
        for h in range(HEADS_PER_BLOCK):
            st = s_sc[h]
            if masked:
                key = lax.broadcasted_iota(I32, (TQ, TQ), 0)
                qry = lax.broadcasted_iota(I32, (TQ, TQ), 1)
                st = jnp.where(key <= qry, st, -jnp.inf)
            m_prev = m_sc[h]
            m_new = jnp.maximum(m_prev, jnp.max(st, axis=0, keepdims=True))
            a = jnp.exp(m_prev - m_new)
            p = jnp.exp(st - m_new)
            l_sc[h] = a * l_sc[h] + jnp.sum(p, axis=0, keepdims=True)
            m_sc[h] = m_new
            vt = vt_ref[0, h * HEAD_DIM:(h + 1) * HEAD_DIM, pl.ds(kc, TQ)]
            acc_sc[h] = a * acc_sc[h] + jnp.dot(vt, p.astype(BF16), preferred_element_type=F32)

    scores_into(0, sa_sc)

    def pair(j, carry):
        scores_into(2 * j + 1, sb_sc)
        consume(2 * j, sa_sc, False)
        scores_into(2 * j + 2, sa_sc)
        consume(2 * j + 1, sb_sc, False)
        return carry

    lax.fori_loop(0, i // 2, pair, 0)

    @pl.when(i % 2 == 0)
    def _():
        consume(i, sa_sc, True)

    @pl.when(i % 2 == 1)
    def _():
        scores_into(i, sb_sc)
        consume(i - 1, sa_sc, False)
        consume(i, sb_sc, True)

    out_t = jnp.concatenate([acc_sc[h] / l_sc[h] for h in range(HEADS_PER_BLOCK)], axis=0)
    o_ref[0] = out_t.T


def _attention_prompt(q, kb, vt, cx):
    b, s, _ = q.shape
    nq = s // TQ
    return pl.pallas_call(
        _attn_prompt_kernel,
        grid=(b, N_HEAD_BLOCKS, nq),
        in_specs=[pl.BlockSpec((1, TQ, LANES), lambda bi, hp, i: (bi, i, hp)),
                  pl.BlockSpec((1, s, LANES), lambda bi, hp, i: (bi, 0, hp)),
                  pl.BlockSpec((1, LANES, s), lambda bi, hp, i: (bi, hp, 0)),
                  pl.BlockSpec((1, 1, s, LANES), lambda bi, hp, i: (bi, hp, 0, 0))],
        out_specs=pl.BlockSpec((1, TQ, LANES), lambda bi, hp, i: (bi, i, hp)),
        out_shape=jax.ShapeDtypeStruct((b, s, D_B), F32),
        scratch_shapes=[pltpu.VMEM((HEADS_PER_BLOCK, 1, TQ), F32),
                        pltpu.VMEM((HEADS_PER_BLOCK, 1, TQ), F32),
                        pltpu.VMEM((HEADS_PER_BLOCK, HEAD_DIM, TQ), F32),
                        pltpu.VMEM((HEADS_PER_BLOCK, TQ, TQ), F32),
                        pltpu.VMEM((HEADS_PER_BLOCK, TQ, TQ), F32)],
        compiler_params=_cparams(("arbitrary", "arbitrary", "arbitrary")),
        name="fox_attention_prompt",
    )(q, kb, vt, cx)


def _attn_sample_kernel(q_ref, kc_ref, vc_ref, kn_ref, vn_ref, cc_ref, cn_ref, o_ref):
    t = q_ref.shape[1]
    p = kc_ref.shape[1]
    sel = _head_select()
    kc_all = kc_ref[0].reshape(p, D_B).astype(BF16)
    vc_all = vc_ref[0].reshape(p, D_B).astype(BF16)
    row = lax.broadcasted_iota(I32, (t, t), 0)
    col = lax.broadcasted_iota(I32, (t, t), 1)
    for hp in range(N_HEAD_BLOCKS):
        lanes = slice(hp * LANES, (hp + 1) * LANES)
        q2 = q_ref[0, :, lanes]
        zero = jnp.zeros_like(q2)
        kc = kc_all[:, lanes]
        vc = vc_all[:, lanes]
        kn = kn_ref[0, :, lanes]
        vn = vn_ref[0, :, lanes]
        outs = []
        for h in range(HEADS_PER_BLOCK):
            qh = jnp.where(sel[h], q2, zero)
            cbase = cn_ref[0, hp, h:h + 1, 0:1]
            s_c = _scores(qh, kc) + (cbase - cc_ref[0, hp, h:h + 1, :])
            s_n = _scores(qh, kn) + (cbase - cn_ref[0, hp, h:h + 1, :])
            s_n = jnp.where(col <= row, s_n, -jnp.inf)
            m = jnp.maximum(jnp.max(s_c, axis=-1, keepdims=True), jnp.max(s_n, axis=-1, keepdims=True))
            p_c = jnp.exp(s_c - m)
            p_n = jnp.exp(s_n - m)
            l = jnp.sum(p_c, axis=-1, keepdims=True) + jnp.sum(p_n, axis=-1, keepdims=True)
            o = (jnp.dot(p_c.astype(BF16), vc, preferred_element_type=F32)
                 + jnp.dot(p_n.astype(BF16), vn, preferred_element_type=F32))
            outs.append(o / l)
        o_ref[0, :, lanes] = jnp.where(sel[0], outs[0], outs[1])


def _attention_sample(q, cache_k, cache_v, kb, vb, c_cache, c_new):
    b, t, _ = q.shape
    p = cache_k.shape[1]
    newblk = pl.BlockSpec((1, t, D_B), lambda bi: (bi, 0, 0))
    cacheblk = pl.BlockSpec((1, p, N_B_HEADS, HEAD_DIM), lambda bi: (bi, 0, 0, 0))
    return pl.pallas_call(
        _attn_sample_kernel,
        grid=(b,),
        in_specs=[newblk, cacheblk, cacheblk, newblk, newblk,
                  pl.BlockSpec((1, N_HEAD_BLOCKS, HEADS_PER_BLOCK, p), lambda bi: (bi, 0, 0, 0)),
                  pl.BlockSpec((1, N_HEAD_BLOCKS, HEADS_PER_BLOCK, t), lambda bi: (bi, 0, 0, 0))],
        out_specs=newblk,
        out_shape=jax.ShapeDtypeStruct((b, t, D_B), F32),
        compiler_params=_cparams(("arbitrary",), vmem=56 * 1024 * 1024),
        name="fox_attention_sample",
    )(q, cache_k, cache_v, kb, vb, c_cache, c_new)


def _tail_kernel(oap_ref, obp_ref, xp_ref, oas_ref, obs_ref, xs_ref, noa_ref, nob_ref, wout_ref,
                 nffn_ref, wr_ref, br_ref,
                 x2_ref, h3_ref, tidx_ref, tw_ref, rank_ref, cnt_ref, carry_sc):
    i = pl.program_id(0)
    is_sample = i == pl.num_programs(0) - 1
    oa = jnp.where(is_sample, oas_ref[...], oap_ref[...])
    ob = jnp.where(is_sample, obs_ref[...], obp_ref[...])
    x = jnp.where(is_sample, xs_ref[...], xp_ref[...])

    ma = (_rms(oa) * noa_ref[...]).astype(BF16)
    mb = (_rms(ob) * nob_ref[...]).astype(BF16)
    x2 = x + (jnp.dot(ma, wout_ref[:D_A, :], preferred_element_type=F32)
              + jnp.dot(mb, wout_ref[D_A:, :], preferred_element_type=F32))
    x2_ref[...] = x2
    h = _rms(x2) * nffn_ref[...]
    for j in range(ROW_TILES):
        h3_ref[_lane_tile(j, TM_TOK), :] = h[:, j * LANES:(j + 1) * LANES]

    lane = lax.broadcasted_iota(I32, (TM_TOK, LANES), 1)
    logits = jnp.dot(h.astype(BF16), wr_ref[...], preferred_element_type=F32) + br_ref[...]
    logits = jnp.where(lane < N_EXPERTS, logits, -jnp.inf)

    lane_f = lane.astype(F32)
    vals, idxs, hots = [], [], []
    work = logits
    for _ in range(TOP_K):
        mx = jnp.max(work, axis=-1, keepdims=True)
        idx = jnp.min(jnp.where(work == mx, lane_f, float(LANES)), axis=-1, keepdims=True)
        hot = lane_f == idx
        vals.append(mx)
        idxs.append(idx.astype(I32))
        hots.append(hot)
        work = jnp.where(hot, -jnp.inf, work)
    exps = [jnp.exp(v - vals[0]) for v in vals]
    denom = exps[0] + exps[1] + exps[2] + exps[3]

    @pl.when(i == 0)
    def _():
        carry_sc[...] = jnp.zeros(carry_sc.shape, F32)

    assigned = jnp.zeros((TM_TOK, LANES), F32)
    for hot in hots:
        assigned = assigned + jnp.where(hot, 1.0, 0.0)
    ri = lax.broadcasted_iota(I32, (TM_TOK, TM_TOK), 0)
    ci = lax.broadcasted_iota(I32, (TM_TOK, TM_TOK), 1)
    earlier = jnp.where(ci < ri, 1.0, 0.0).astype(BF16)
    before = jnp.dot(earlier, assigned.astype(BF16), preferred_element_type=F32) + carry_sc[...]

    tidx = jnp.zeros((TM_TOK, LANES), I32)
    tw = jnp.zeros((TM_TOK, LANES), F32)
    rank = jnp.zeros((TM_TOK, LANES), I32)
    for k in range(TOP_K):
        rk = jnp.sum(jnp.where(hots[k], before, 0.0), axis=-1, keepdims=True).astype(I32)
        tidx = jnp.where(lane == k, idxs[k], tidx)
        tw = jnp.where(lane == k, exps[k] / denom, tw)
        rank = jnp.where(lane == k, rk, rank)
    tidx_ref[...] = tidx
    tw_ref[...] = tw
    rank_ref[...] = rank
    total = carry_sc[...] + jnp.sum(assigned, axis=0, keepdims=True)
    carry_sc[...] = total
    cnt_ref[...] = jnp.broadcast_to(total, cnt_ref.shape)


def _tail(oa_p, ob_p, x_p, oa_s, ob_s, x_s, noa, nob, wout_b, nffn, wr_b, br_pad):
    n_prompt_tiles = x_p.shape[0] // TM_TOK
    n_tiles = n_prompt_tiles + 1
    n_pad = n_tiles * TM_TOK
    pblk = lambda w: pl.BlockSpec((TM_TOK, w), lambda i: (jnp.minimum(i, n_prompt_tiles - 1), 0))
    full = lambda shape: pl.BlockSpec(shape, lambda i: (0,) * len(shape))
    oblk = lambda w: pl.BlockSpec((TM_TOK, w), lambda i: (i, 0))
    out_shape = (
        jax.ShapeDtypeStruct((n_pad, D_MODEL), F32),
        jax.ShapeDtypeStruct((n_pad * ROW_TILES, LANES), F32),
        jax.ShapeDtypeStruct((n_pad, LANES), I32),
        jax.ShapeDtypeStruct((n_pad, LANES), F32),
        jax.ShapeDtypeStruct((n_pad, LANES), I32),
        jax.ShapeDtypeStruct((SUBLANES, LANES), F32),
    )
    return pl.pallas_call(
        _tail_kernel,
        grid=(n_tiles,),
        in_specs=[pblk(D_A), pblk(D_B), pblk(D_MODEL), full((TM_TOK, D_A)), full((TM_TOK, D_B)),
                  full((TM_TOK, D_MODEL)), full((1, D_A)), full((1, D_B)), full((D_MODEL, D_MODEL)),
                  full((1, D_MODEL)), full((D_MODEL, LANES)), full((1, LANES))],
        out_specs=(oblk(D_MODEL), pl.BlockSpec((TM_TOK * ROW_TILES, LANES), lambda i: (i, 0)),
                   oblk(LANES), oblk(LANES), oblk(LANES), full((SUBLANES, LANES))),
        out_shape=out_shape,
        scratch_shapes=[pltpu.VMEM((1, LANES), F32)],
        compiler_params=_cparams(("arbitrary",)),
        name="outproj_router",
    )(oa_p, ob_p, x_p, oa_s, ob_s, x_s, noa, nob, wout_b, nffn, wr_b, br_pad)


def _row_copy(src, src_row, dst, dst_row, sem):
    return pltpu.make_async_copy(src.at[_tile_rows(src_row, 1)], dst.at[_tile_rows(dst_row, 1)], sem)


N_ASSIGN = TM_TOK * TOP_K
SLOT_RING = 3
ISSUE_UNROLL = 4


def _slot_copy(slot_hbm, slot_sm, ssem, tile):
    place = tile % SLOT_RING
    return pltpu.make_async_copy(slot_hbm.at[pl.ds(pl.multiple_of(tile * N_ASSIGN, N_ASSIGN), N_ASSIGN)],
                                 slot_sm.at[pl.ds(pl.multiple_of(place * N_ASSIGN, N_ASSIGN), N_ASSIGN)],
                                 ssem.at[place])


def _for_each_assignment(slot_sm, tile, fn):
    base = (tile % SLOT_RING) * N_ASSIGN

    def body(g, carry):
        first = g * ISSUE_UNROLL
        slots = [slot_sm[base + first * TOP_K + a] for a in range(ISSUE_UNROLL * TOP_K)]
        for u in range(ISSUE_UNROLL):
            for k in range(TOP_K):
                fn(first + u, k, slots[u * TOP_K + k])
        return carry

    lax.fori_loop(0, TM_TOK // ISSUE_UNROLL, body, 0)


def _dispatch_kernel(pad_start_ref, pad_len_ref, nu_ref, slot_hbm, h3_hbm, xs_hbm,
                     slot_sm, stage, zblk, sem, ssem, hsem, zsem):
    i = pl.program_id(0)
    n = pl.num_programs(0)

    def stage_copy(tile):
        place = tile % SLOT_RING
        return pltpu.make_async_copy(h3_hbm.at[_tile_rows(tile * TM_TOK, TM_TOK)],
                                     stage.at[_tile_rows(place * TM_TOK, TM_TOK)], hsem.at[place])

    @pl.when(i == 0)
    def _():
        _slot_copy(slot_hbm, slot_sm, ssem, 0).start()
        stage_copy(0).start()

    @pl.when(i + 1 < n)
    def _():
        _slot_copy(slot_hbm, slot_sm, ssem, i + 1).start()
        stage_copy(i + 1).start()

    _slot_copy(slot_hbm, slot_sm, ssem, i).wait()
    stage_copy(i).wait()

    def row_copy(tile, t, slot):
        return _row_copy(stage, (tile % SLOT_RING) * TM_TOK + t, xs_hbm, slot, sem.at[tile % 2])

    _for_each_assignment(slot_sm, i, lambda t, k, slot: row_copy(i, t, slot).start(priority=k % 2))

    @pl.when(i > 0)
    def _():
        _for_each_assignment(slot_sm, i - 1, lambda t, k, slot: row_copy(i - 1, t, slot).wait())

    @pl.when(i == n - 1)
    def _():
        _for_each_assignment(slot_sm, i, lambda t, k, slot: row_copy(i, t, slot).wait())
        zblk[...] = jnp.zeros(zblk.shape, F32)
        for e in range(N_EXPERTS):
            start = pad_start_ref[e]
            n_pad_rows = pad_len_ref[e]

            def zissue(r, carry):
                _row_copy(zblk, 0, xs_hbm, start + r, zsem).start()
                return carry

            def zdrain(r, carry):
                _row_copy(zblk, 0, xs_hbm, start + r, zsem).wait()
                return carry

            lax.fori_loop(0, n_pad_rows, zissue, 0)
            lax.fori_loop(0, n_pad_rows, zdrain, 0)

        def tile_copy(r):
            return pltpu.make_async_copy(zblk, xs_hbm.at[_tile_rows(r * TM_MOE, TM_MOE)], zsem)

        def tissue(r, carry):
            tile_copy(r).start()
            return carry

        def tdrain(r, carry):
            tile_copy(r).wait()
            return carry

        n_tiles_moe = xs_hbm.shape[0] // (TM_MOE * ROW_TILES)
        lax.fori_loop(nu_ref[0], n_tiles_moe, tissue, 0)
        lax.fori_loop(nu_ref[0], n_tiles_moe, tdrain, 0)


def _dispatch(pad_start, pad_len, n_used, slot_flat, h3, n_slots):
    n_tiles = h3.shape[0] // (TM_TOK * ROW_TILES)
    grid_spec = pltpu.PrefetchScalarGridSpec(
        num_scalar_prefetch=3,
        grid=(n_tiles,),
        in_specs=[pl.BlockSpec(memory_space=pl.ANY), pl.BlockSpec(memory_space=pl.ANY)],
        out_specs=pl.BlockSpec(memory_space=pl.ANY),
        scratch_shapes=[pltpu.SMEM((SLOT_RING * N_ASSIGN,), I32),
                        pltpu.VMEM((SLOT_RING * TM_TOK * ROW_TILES, LANES), F32),
                        pltpu.VMEM((TM_MOE * ROW_TILES, LANES), F32),
                        pltpu.SemaphoreType.DMA((2,)), pltpu.SemaphoreType.DMA((SLOT_RING,)),
                        pltpu.SemaphoreType.DMA((SLOT_RING,)), pltpu.SemaphoreType.DMA],
    )
    return pl.pallas_call(
        _dispatch_kernel,
        grid_spec=grid_spec,
        out_shape=jax.ShapeDtypeStruct((n_slots * ROW_TILES, LANES), F32),
        compiler_params=_cparams(("arbitrary",)),
        name="moe_dispatch",
    )(pad_start, pad_len, n_used, slot_flat, h3)


def _moe_kernel(te_ref, nu_ref, xs_ref, w1_ref, b1_ref, w2_ref, b2_ref, ys_ref, w1b_sc, w2b_sc):
    i = pl.program_id(0)
    prev = te_ref[jnp.maximum(i - 1, 0)]
    new_expert = jnp.logical_or(i == 0, te_ref[i] != prev)

    @pl.when(new_expert)
    def _():
        blk = 128
        for r in range(D_MODEL // blk):
            w1b_sc[r * blk:(r + 1) * blk, :] = w1_ref[0, r * blk:(r + 1) * blk, :].astype(BF16)
        for r in range(D_FF // blk):
            w2b_sc[r * blk:(r + 1) * blk, :] = w2_ref[0, r * blk:(r + 1) * blk, :].astype(BF16)

    @pl.when(i < nu_ref[0])
    def _():
        xb = jnp.concatenate([xs_ref[_lane_tile(j, TM_MOE), :] for j in range(ROW_TILES)], axis=-1).astype(BF16)
        acc = jnp.zeros((TM_MOE, D_MODEL), F32)
        for c in range(D_FF // FF_BLK):
            lo = c * FF_BLK
            glu = jnp.dot(xb, w1b_sc[:, lo:lo + FF_BLK], preferred_element_type=F32) + b1_ref[0, :, lo:lo + FF_BLK]
            lin = (jnp.dot(xb, w1b_sc[:, D_FF + lo:D_FF + lo + FF_BLK], preferred_element_type=F32)
                   + b1_ref[0, :, D_FF + lo:D_FF + lo + FF_BLK])
            glu = jnp.minimum(glu, SWIGLU_LIMIT)
            lin = jnp.clip(lin, -SWIGLU_LIMIT, SWIGLU_LIMIT)
            act = glu * _sigmoid(SWIGLU_ALPHA * glu) * (lin + 1.0)
            acc = acc + jnp.dot(act.astype(BF16), w2b_sc[lo:lo + FF_BLK, :], preferred_element_type=F32)
        out = acc + b2_ref[0]
        for j in range(ROW_TILES):
            ys_ref[_lane_tile(j, TM_MOE), :] = out[:, j * LANES:(j + 1) * LANES]

    @pl.when(i >= nu_ref[0])
    def _():
        ys_ref[...] = jnp.zeros(ys_ref.shape, F32)


def _moe(tile_expert, n_used, xs, w1, b1, w2, b2):
    n_tiles = xs.shape[0] // (TM_MOE * ROW_TILES)
    exp_map = lambda i, te, nu: (te[i], 0, 0)
    grid_spec = pltpu.PrefetchScalarGridSpec(
        num_scalar_prefetch=2,
        grid=(n_tiles,),
        in_specs=[pl.BlockSpec((TM_MOE * ROW_TILES, LANES), lambda i, te, nu: (jnp.minimum(i, nu[0] - 1), 0)),
                  pl.BlockSpec((1, D_MODEL, 2 * D_FF), exp_map),
                  pl.BlockSpec((1, 1, 2 * D_FF), exp_map),
                  pl.BlockSpec((1, D_FF, D_MODEL), exp_map),
                  pl.BlockSpec((1, 1, D_MODEL), exp_map)],
        out_specs=pl.BlockSpec((TM_MOE * ROW_TILES, LANES), lambda i, te, nu: (i, 0)),
        scratch_shapes=[pltpu.VMEM((D_MODEL, 2 * D_FF), BF16), pltpu.VMEM((D_FF, D_MODEL), BF16)],
    )
    return pl.pallas_call(
        _moe_kernel,
        grid_spec=grid_spec,
        out_shape=jax.ShapeDtypeStruct(xs.shape, F32),
        compiler_params=_cparams(("arbitrary",), vmem=56 * 1024 * 1024),
        name="moe_experts",
    )(tile_expert, n_used, xs, w1, b1, w2, b2)


def _combine_kernel(slot_hbm, ys_hbm, x2_ref, tw_ref, nf_ref, yp_ref, ys_ref, slot_sm, buf_a, buf_b, sem, ssem):
    i = pl.program_id(0)
    n = pl.num_programs(0)

    def row_copy(tile, buf, t, k, slot):
        return _row_copy(ys_hbm, slot, buf, k * TM_TOK + t, sem.at[tile % 2])

    def start_gather(tile, buf):
        _for_each_assignment(slot_sm, tile, lambda t, k, slot: row_copy(tile, buf, t, k, slot).start(priority=k % 2))

    @pl.when(i == 0)
    def _():
        first = _slot_copy(slot_hbm, slot_sm, ssem, 0)
        first.start()
        first.wait()
        start_gather(0, buf_a)
        _slot_copy(slot_hbm, slot_sm, ssem, 1).start()

    @pl.when(i + 2 < n)
    def _():
        _slot_copy(slot_hbm, slot_sm, ssem, i + 2).start()

    def work(buf, other):
        @pl.when(i + 1 < n)
        def _():
            _slot_copy(slot_hbm, slot_sm, ssem, i + 1).wait()
            start_gather(i + 1, other)

        _for_each_assignment(slot_sm, i, lambda t, k, slot: row_copy(i, buf, t, k, slot).wait())
        tw = tw_ref[...]
        parts = []
        for j in range(ROW_TILES):
            acc = jnp.zeros((TM_TOK, LANES), F32)
            for k in range(TOP_K):
                acc = acc + tw[:, k:k + 1] * buf[_lane_tile(j, TM_TOK, first_row=k * TM_TOK), :]
            parts.append(acc)
        y = _rms(x2_ref[...] + jnp.concatenate(parts, axis=-1)) * nf_ref[...]

        @pl.when(i < n - 1)
        def _():
            yp_ref[...] = y

        @pl.when(i == n - 1)
        def _():
            ys_ref[...] = y

    @pl.when(i % 2 == 0)
    def _():
        work(buf_a, buf_b)

    @pl.when(i % 2 == 1)
    def _():
        work(buf_b, buf_a)


def _combine(slot_flat, ys, x2, tw, norm_final):
    n_tiles = x2.shape[0] // TM_TOK
    n_prompt_tiles = n_tiles - 1
    return pl.pallas_call(
        _combine_kernel,
        grid=(n_tiles,),
        in_specs=[pl.BlockSpec(memory_space=pl.ANY), pl.BlockSpec(memory_space=pl.ANY),
                  pl.BlockSpec((TM_TOK, D_MODEL), lambda i: (i, 0)),
                  pl.BlockSpec((TM_TOK, LANES), lambda i: (i, 0)),
                  pl.BlockSpec((1, D_MODEL), lambda i: (0, 0))],
        out_specs=(pl.BlockSpec((TM_TOK, D_MODEL), lambda i: (jnp.minimum(i, n_prompt_tiles - 1), 0)),
                   pl.BlockSpec((TM_TOK, D_MODEL), lambda i: (0, 0))),
        out_shape=(jax.ShapeDtypeStruct((n_prompt_tiles * TM_TOK, D_MODEL), F32),
                   jax.ShapeDtypeStruct((TM_TOK, D_MODEL), F32)),
        scratch_shapes=[pltpu.SMEM((SLOT_RING * N_ASSIGN,), I32),
                        pltpu.VMEM((TOP_K * TM_TOK * ROW_TILES, LANES), F32),
                        pltpu.VMEM((TOP_K * TM_TOK * ROW_TILES, LANES), F32),
                        pltpu.SemaphoreType.DMA((2,)), pltpu.SemaphoreType.DMA((SLOT_RING,))],
        compiler_params=_cparams(("arbitrary",)),
        name="moe_combine",
    )(slot_flat, ys, x2, tw, norm_final)


def _head_pairs(c_bht):
    b, h, t = c_bht.shape
    return c_bht.reshape(b, N_HEAD_BLOCKS, HEADS_PER_BLOCK, t)


def _bias_lanes(pieces_bht):
    b, h, t = pieces_bht[0].shape
    x = jnp.stack(pieces_bht, axis=-1).astype(BF16)
    x = x.reshape(b, N_HEAD_BLOCKS, HEADS_PER_BLOCK, t, BIAS_PIECES).transpose(0, 1, 3, 2, 4)
    x = x.reshape(b, N_HEAD_BLOCKS, t, HEADS_PER_BLOCK * BIAS_PIECES)
    return jnp.pad(x, ((0, 0), (0, 0), (0, 0), (0, LANES - HEADS_PER_BLOCK * BIAS_PIECES)))


def _routing_tables(tidx, rank, counts, n_tiles_moe):
    padded = ((counts + TM_MOE - 1) // TM_MOE) * TM_MOE
    ends = jnp.cumsum(padded)
    offs = ends - padded
    experts = jnp.arange(N_EXPERTS, dtype=I32)
    off_of = jnp.sum(jnp.where(tidx[..., None] == experts, offs, 0), axis=-1)
    slot = (off_of + rank).reshape(-1).astype(I32)
    n_used = (ends[-1] // TM_MOE).astype(I32)
    tile_start = jnp.arange(n_tiles_moe, dtype=I32) * TM_MOE
    te = jnp.sum((tile_start[:, None] >= ends[None, :]).astype(I32), axis=-1)
    te_last = jnp.sum(((n_used - 1) * TM_MOE >= ends).astype(I32))
    te = jnp.minimum(jnp.where(jnp.arange(n_tiles_moe) < n_used, te, te_last), N_EXPERTS - 1).astype(I32)
    return slot, te, n_used.reshape(1), (offs + counts).astype(I32), (padded - counts).astype(I32)


def kernel(x_prompt, x_sample, cache_k, cache_v, cache_logf, norm_mix, w_in, b_f, norm_av, w_s, b_s,
           norm_out_a, norm_out_b, w_out, norm_ffn, w_router, b_router, w1, b1, w2, b2, norm_final):
    depth = w_in.shape[0]
    bsz, seq, _ = x_prompt.shape
    dbsz, dseq, _ = x_sample.shape
    past = cache_k.shape[2]
    n_prompt = bsz * seq
    n_sample = dbsz * dseq
    assert depth == 1, "the combine kernel applies the final norm, so it must follow the only layer"
    assert seq % TM_PROJ == 0 and n_prompt % TM_TOK == 0 and seq % TQ == 0
    assert n_sample == MLP_CHUNK and MLP_CHUNK % dseq == 0 and n_sample <= TM_TOK
    n_pad = n_prompt + TM_TOK
    n_tiles_moe = (n_pad * TOP_K) // TM_MOE + N_EXPERTS
    n_slots = n_tiles_moe * TM_MOE

    xp = x_prompt.reshape(n_prompt, D_MODEL)
    xs = x_sample.reshape(n_sample, D_MODEL)
    outs = {k: [] for k in ("kp", "vp", "lfp", "ks", "vs", "lfs", "gvs")}
    for l in range(depth):
        win_b = jnp.pad(w_in[l], ((0, 0), (0, W_IN_COLS - w_in.shape[2]))).astype(BF16)
        bf_pad = jnp.pad(b_f[l], (0, LANES - N_B_HEADS)).reshape(1, LANES)
        nm = norm_mix[l].reshape(1, D_MODEL)
        nav = norm_av[l].reshape(1, D_A)
        reps = MLP_CHUNK // dseq
        ws_s = jnp.tile(w_s[l][:, :dseq, :dseq], (1, reps, reps))
        bst_p = b_s[l].T
        bst_s = jnp.tile(b_s[l][:, :dseq].T, (reps, 1))

        oa_p, q_p, k_p, v_p, kb_p, vt_p, lf_p, _ = _in_projection(
            xp, nm, win_b, bf_pad, nav, w_s[l], bst_p, tm=TM_PROJ, period=MLP_CHUNK, seq_for_vt=seq)
        lf_p3 = lf_p.reshape(bsz, seq, N_B_HEADS)
        _, *neg_c_pieces = _cumsum_time(lf_p3.transpose(0, 2, 1))
        ob_p = _attention_prompt(q_p.reshape(bsz, seq, D_B), kb_p.reshape(bsz, seq, D_B),
                                 vt_p, _bias_lanes(neg_c_pieces))

        oa_s, q_s, k_s, v_s, kb_s, vb_s, lf_s, vn_s = _in_projection(
            xs, nm, win_b, bf_pad, nav, ws_s, bst_s, tm=n_sample, period=dseq)
        lf_s3 = lf_s.reshape(dbsz, dseq, N_B_HEADS)
        t_all = past + dseq
        t_padded = -(-t_all // CS_BLK) * CS_BLK
        lf_all = jnp.concatenate([cache_logf[l].astype(F32), lf_s3], axis=1).transpose(0, 2, 1)
        c_all = _cumsum_time(jnp.pad(lf_all, ((0, 0), (0, 0), (0, t_padded - t_all))))[0]
        ob_s = _attention_sample(q_s.reshape(dbsz, dseq, D_B),
                                 cache_k[l], cache_v[l],
                                 kb_s.reshape(dbsz, dseq, D_B), vb_s.reshape(dbsz, dseq, D_B),
                                 _head_pairs(c_all[:, :, :past]), _head_pairs(c_all[:, :, past:t_all]))

        pad_rows = lambda a: jnp.pad(a, ((0, TM_TOK - n_sample), (0, 0)))
        wr_b = jnp.pad(w_router[l], ((0, 0), (0, LANES - N_EXPERTS))).astype(BF16)
        br_pad = jnp.pad(b_router[l], (0, LANES - N_EXPERTS)).reshape(1, LANES)
        x2, h3, tidx, tw, rank, cnt = _tail(
            oa_p, ob_p.reshape(n_prompt, D_B), xp, pad_rows(oa_s), pad_rows(ob_s.reshape(n_sample, D_B)),
            pad_rows(xs), norm_out_a[l].reshape(1, D_A), norm_out_b[l].reshape(1, D_B), w_out[l].astype(BF16),
            norm_ffn[l].reshape(1, D_MODEL), wr_b, br_pad)
        counts = cnt[0, :N_EXPERTS].astype(I32)
        slot, te, n_used, pad_start, pad_len = _routing_tables(tidx[:, :TOP_K], rank[:, :TOP_K], counts, n_tiles_moe)
        xsort = _dispatch(pad_start, pad_len, n_used, slot, h3, n_slots)
        ysort = _moe(te, n_used, xsort, w1[l], b1[l].reshape(N_EXPERTS, 1, 2 * D_FF), w2[l],
                     b2[l].reshape(N_EXPERTS, 1, D_MODEL))
        xp, y_tail = _combine(slot, ysort, x2, tw, norm_final.reshape(1, D_MODEL))
        xs = y_tail[:n_sample]

        outs["kp"].append(k_p.reshape(bsz, seq, N_B_HEADS, HEAD_DIM))
        outs["vp"].append(v_p.reshape(bsz, seq, N_B_HEADS, HEAD_DIM))
        outs["lfp"].append(lf_p3)
        outs["ks"].append(k_s.reshape(dbsz, dseq, N_B_HEADS, HEAD_DIM))
        outs["vs"].append(v_s.reshape(dbsz, dseq, N_B_HEADS, HEAD_DIM))
        outs["lfs"].append(lf_s3)
        outs["gvs"].append(vn_s.reshape(dbsz, dseq, D_A))

    y_prompt = xp.reshape(bsz, seq, D_MODEL)
    y_sample = xs.reshape(dbsz, dseq, D_MODEL)
    st = jnp.stack
    return (y_prompt, y_sample, st(outs["kp"]), st(outs["vp"]), st(outs["lfp"]),
            st(outs["ks"]), st(outs["vs"]), st(outs["lfs"]), st(outs["gvs"]))
```

```python
import functools

import jax
import jax.numpy as jnp
import numpy as np
from jax import lax
from jax.experimental import pallas as pl
from jax.experimental.pallas import tpu as pltpu

F32 = jnp.float32
BF16 = jnp.bfloat16
I32 = jnp.int32

D_MODEL = 1024
CHUNK = 64
MLP_CHUNK = 128
D_A = 512
A_GROUPS = 4
A_GROUP_CH = D_A // A_GROUPS
D_B = 512
N_B_HEADS = 8
HEAD_DIM = D_B // N_B_HEADS
N_EXPERTS = 32
TOP_K = 4
D_FF = D_MODEL
SWIGLU_ALPHA = 1.702
SWIGLU_LIMIT = 7.0
RMS_EPS = 1e-6

LANES = 128
SUBLANES = 8
ROW_TILES = D_MODEL // LANES
HEADS_PER_BLOCK = LANES // HEAD_DIM
N_HEAD_BLOCKS = N_B_HEADS // HEADS_PER_BLOCK
QKV_OFF = 2 * D_A
F_OFF = 2 * D_A + 3 * D_B
W_IN_COLS = F_OFF + LANES

TM_PROJ = 512
TM_TOK = 256
TM_MOE = 256
TQ = 256
TKB = 1024
KCH = 512
BIAS_PIECES = 3
CS_BLK = 256
FF_BLK = 512
VMEM_LIMIT = 48 * 1024 * 1024


def _cparams(sem, vmem=VMEM_LIMIT):
    return pltpu.CompilerParams(dimension_semantics=sem, vmem_limit_bytes=vmem)


def _gelu(x):
    return 0.5 * x * (1.0 + lax.erf(x * 0.7071067811865476))


def _log_sigmoid(x):
    return jnp.minimum(x, 0.0) - jnp.log1p(jnp.exp(-jnp.abs(x)))


def _sigmoid(x):
    return 1.0 / (1.0 + jnp.exp(-x))


def _rms(x):
    return x * lax.rsqrt(jnp.mean(x * x, axis=-1, keepdims=True) + RMS_EPS)


def _tile_rows(first_row, n_rows):
    return pl.ds(pl.multiple_of(first_row * ROW_TILES, ROW_TILES), n_rows * ROW_TILES)


def _lane_tile(j, n_rows, first_row=0):
    return pl.ds(first_row * ROW_TILES + j, n_rows, stride=ROW_TILES)


def _div_pow2(x, d):
    assert d & (d - 1) == 0
    return jnp.right_shift(x, d.bit_length() - 1)


def _mod_pow2(x, d):
    assert d & (d - 1) == 0
    return jnp.bitwise_and(x, d - 1)


def _store_heads(ref, x):
    ref[...] = x.reshape(x.shape[0], N_B_HEADS, HEAD_DIM)


def _inproj_kernel(x_ref, nm_ref, win_ref, bf_ref, nav_ref, ws_ref, bst_ref,
                   outa_ref, q_ref, k_ref, v_ref, kb_ref, vb_ref, logf_ref, vn_ref, *, period, v_transposed):
    tm = x_ref.shape[0]
    hn = (_rms(x_ref[...]) * nm_ref[...]).astype(BF16)

    def proj(lo, width):
        return jnp.dot(hn, win_ref[:, lo:lo + width], preferred_element_type=F32)

    u = _gelu(proj(0, D_A))
    vn = _rms(_gelu(proj(D_A, D_A))) * nav_ref[...]
    vn_ref[...] = vn
    q_ref[...] = (proj(QKV_OFF, D_B) * (HEAD_DIM ** -0.5)).astype(BF16)
    kk = proj(QKV_OFF + D_B, D_B)
    _store_heads(k_ref, kk)
    kb_ref[...] = kk.astype(BF16)
    vv = proj(QKV_OFF + 2 * D_B, D_B)
    _store_heads(v_ref, vv)
    if v_transposed:
        vb_ref[0] = vv.T.astype(BF16)
    else:
        vb_ref[...] = vv.astype(BF16)
    fl = proj(F_OFF, LANES) + bf_ref[...]
    logf_ref[...] = _log_sigmoid(fl)[:, :N_B_HEADS]

    vnb = vn.astype(BF16)
    ri = lax.broadcasted_iota(I32, (MLP_CHUNK, MLP_CHUNK), 0)
    ci = lax.broadcasted_iota(I32, (MLP_CHUNK, MLP_CHUNK), 1)
    same_stream = _div_pow2(ri, period) == _div_pow2(ci, period)
    visible = _div_pow2(_mod_pow2(ci, period), CHUNK) <= _div_pow2(_mod_pow2(ri, period), CHUNK)
    mask = jnp.logical_and(same_stream, visible)
    for g in range(A_GROUPS):
        w = jnp.where(mask, ws_ref[g], 0.0).astype(BF16)
        bcol = bst_ref[:, g:g + 1]
        cols = slice(g * A_GROUP_CH, (g + 1) * A_GROUP_CH)
        for c in range(tm // MLP_CHUNK):
            rows = slice(c * MLP_CHUNK, (c + 1) * MLP_CHUNK)
            mixed = jnp.dot(w, vnb[rows, cols], preferred_element_type=F32) + bcol
            outa_ref[rows, cols] = u[rows, cols] * mixed


def _in_projection(x2d, norm_mix, win_b, bf_pad, norm_av, ws_tiled, bst_tiled, *, tm, period, seq_for_vt=None):
    rows = x2d.shape[0]
    row_blk = lambda w: pl.BlockSpec((tm, w), lambda i: (i, 0))
    full = lambda shape: pl.BlockSpec(shape, lambda i: (0,) * len(shape))
    head_blk = pl.BlockSpec((tm, N_B_HEADS, HEAD_DIM), lambda i: (i, 0, 0))
    if seq_for_vt is None:
        vb_shape, vb_blk = jax.ShapeDtypeStruct((rows, D_B), BF16), row_blk(D_B)
    else:
        tiles = seq_for_vt // tm
        vb_shape = jax.ShapeDtypeStruct((rows // seq_for_vt, D_B, seq_for_vt), BF16)
        vb_blk = pl.BlockSpec((1, D_B, tm), lambda i: (i // tiles, 0, i % tiles))
    out_shape = (
        jax.ShapeDtypeStruct((rows, D_A), F32),
        jax.ShapeDtypeStruct((rows, D_B), BF16),
        jax.ShapeDtypeStruct((rows, N_B_HEADS, HEAD_DIM), F32),
        jax.ShapeDtypeStruct((rows, N_B_HEADS, HEAD_DIM), F32),
        jax.ShapeDtypeStruct((rows, D_B), BF16),
        vb_shape,
        jax.ShapeDtypeStruct((rows, N_B_HEADS), F32),
        jax.ShapeDtypeStruct((rows, D_A), F32),
    )
    return pl.pallas_call(
        functools.partial(_inproj_kernel, period=period, v_transposed=seq_for_vt is not None),
        grid=(rows // tm,),
        in_specs=[row_blk(D_MODEL), full((1, D_MODEL)), full((D_MODEL, W_IN_COLS)), full((1, LANES)),
                  full((1, D_A)), full((A_GROUPS, MLP_CHUNK, MLP_CHUNK)), full((MLP_CHUNK, A_GROUPS))],
        out_specs=(row_blk(D_A), row_blk(D_B), head_blk, head_blk, row_blk(D_B), vb_blk,
                   row_blk(N_B_HEADS), row_blk(D_A)),
        out_shape=out_shape,
        compiler_params=_cparams(("arbitrary",)),
        name="in_projection",
    )(x2d, norm_mix, win_b, bf_pad, norm_av, ws_tiled, bst_tiled)


def _split_bf16(x):
    x1 = x.astype(BF16)
    r1 = x - x1.astype(F32)
    x2 = r1.astype(BF16)
    x3 = (r1 - x2.astype(F32)).astype(BF16)
    return x1, x2, x3


def _cumsum_kernel(x_ref, o_ref, *piece_refs):
    t = x_ref.shape[2]
    ri = lax.broadcasted_iota(I32, (CS_BLK, CS_BLK), 0)
    ci = lax.broadcasted_iota(I32, (CS_BLK, CS_BLK), 1)
    upper = jnp.where(ri <= ci, 1.0, 0.0).astype(BF16)
    carry = jnp.zeros((N_B_HEADS, 1), F32)
    for c in range(t // CS_BLK):
        cols = slice(c * CS_BLK, (c + 1) * CS_BLK)
        s = sum(jnp.dot(piece, upper, preferred_element_type=F32) for piece in _split_bf16(x_ref[0, :, cols])) + carry
        o_ref[0, :, cols] = s
        for ref, piece in zip(piece_refs, _split_bf16(-s)):
            ref[0, :, cols] = piece.astype(F32)
        carry = s[:, CS_BLK - 1:CS_BLK]


def _cumsum_time(x_bht):
    b, h, t = x_bht.shape
    blk = pl.BlockSpec((1, h, t), lambda i: (i, 0, 0))
    return pl.pallas_call(
        _cumsum_kernel,
        grid=(b,),
        in_specs=[blk],
        out_specs=(blk,) * (1 + BIAS_PIECES),
        out_shape=(jax.ShapeDtypeStruct((b, h, t), F32),) * (1 + BIAS_PIECES),
        compiler_params=_cparams(("arbitrary",)),
        name="cumsum_logf",
    )(x_bht)


def _head_select(width=LANES):
    lane = lax.broadcasted_iota(I32, (1, width), 1)
    return [jnp.logical_and(lane >= h * HEAD_DIM, lane < (h + 1) * HEAD_DIM) for h in range(HEADS_PER_BLOCK)]


def _scores(qh, k2):
    return lax.dot_general(qh, k2, (((1,), (1,)), ((), ())), preferred_element_type=F32)


def _attn_prompt_kernel(q_ref, k_ref, vt_ref, cx_ref, o_ref, m_sc, l_sc, acc_sc, sa_sc, sb_sc):
    i = pl.program_id(2)
    lane = lax.broadcasted_iota(I32, (1, LANES), 1)
    q2 = q_ref[0]
    qaug = []
    for h in range(HEADS_PER_BLOCK):
        head = jnp.logical_and(lane >= h * HEAD_DIM, lane < (h + 1) * HEAD_DIM)
        pick = jnp.logical_and(lane >= h * BIAS_PIECES, lane < (h + 1) * BIAS_PIECES)
        head_row = jnp.where(head, 1.0, 0.0).astype(BF16)
        pick_row = jnp.where(pick, 1.0, 0.0).astype(BF16)
        qaug.append(jnp.concatenate([q2 * head_row, jnp.broadcast_to(pick_row, q2.shape)], axis=1))

    m_sc[...] = jnp.full(m_sc.shape, -jnp.inf, F32)
    l_sc[...] = jnp.zeros(l_sc.shape, F32)
    acc_sc[...] = jnp.zeros(acc_sc.shape, F32)

    def scores_into(c, s_sc):
        kc = pl.multiple_of(c * TQ, TQ)
        kk = jnp.concatenate([k_ref[0, pl.ds(kc, TQ), :], cx_ref[0, 0, pl.ds(kc, TQ), :]], axis=1)
        for h in range(HEADS_PER_BLOCK):
            s_sc[h] = lax.dot_general(kk, qaug[h], (((1,), (1,)), ((), ())), preferred_element_type=F32)

    def consume(c, s_sc, masked):
        kc = pl.multiple_of(c * TQ, TQ)
        for h in range(HEADS_PER_BLOCK):
            st = s_sc[h]
            if masked:
                key = lax.broadcasted_iota(I32, (TQ, TQ), 0)
                qry = lax.broadcasted_iota(I32, (TQ, TQ), 1)
                st = jnp.where(key <= qry, st, -jnp.inf)
            m_prev = m_sc[h]
            m_new = jnp.maximum(m_prev, jnp.max(st, axis=0, keepdims=True))
            a = jnp.exp(m_prev - m_new)
            p = jnp.exp(st - m_new)
            l_sc[h] = a * l_sc[h] + jnp.sum(p, axis=0, keepdims=True)
            m_sc[h] = m_new
            vt = vt_ref[0, h * HEAD_DIM:(h + 1) * HEAD_DIM, pl.ds(kc, TQ)]
            acc_sc[h] = a * acc_sc[h] + jnp.dot(vt, p.astype(BF16), preferred_element_type=F32)

    scores_into(0, sa_sc)

    def pair(j, carry):
        scores_into(2 * j + 1, sb_sc)
        consume(2 * j, sa_sc, False)
        scores_into(2 * j + 2, sa_sc)
        consume(2 * j + 1, sb_sc, False)
        return carry

    lax.fori_loop(0, i // 2, pair, 0)

    @pl.when(i % 2 == 0)
    def _():
        consume(i, sa_sc, True)

    @pl.when(i % 2 == 1)
    def _():
        scores_into(i, sb_sc)
        consume(i - 1, sa_sc, False)
        consume(i, sb_sc, True)

    out_t = jnp.concatenate([acc_sc[h] / l_sc[h] for h in range(HEADS_PER_BLOCK)], axis=0)
    o_ref[0] = out_t.T


def _attention_prompt(q, kb, vt, cx):
    b, s, _ = q.shape
    nq = s // TQ
    return pl.pallas_call(
        _attn_prompt_kernel,
        grid=(b, N_HEAD_BLOCKS, nq),
        in_specs=[pl.BlockSpec((1, TQ, LANES), lambda bi, hp, i: (bi, i, hp)),
                  pl.BlockSpec((1, s, LANES), lambda bi, hp, i: (bi, 0, hp)),
                  pl.BlockSpec((1, LANES, s), lambda bi, hp, i: (bi, hp, 0)),
                  pl.BlockSpec((1, 1, s, LANES), lambda bi, hp, i: (bi, hp, 0, 0))],
        out_specs=pl.BlockSpec((1, TQ, LANES), lambda bi, hp, i: (bi, i, hp)),
        out_shape=jax.ShapeDtypeStruct((b, s, D_B), F32),
        scratch_shapes=[pltpu.VMEM((HEADS_PER_BLOCK, 1, TQ), F32),
                        pltpu.VMEM((HEADS_PER_BLOCK, 1, TQ), F32),
                        pltpu.VMEM((HEADS_PER_BLOCK, HEAD_DIM, TQ), F32),
                        pltpu.VMEM((HEADS_PER_BLOCK, TQ, TQ), F32),
                        pltpu.VMEM((HEADS_PER_BLOCK, TQ, TQ), F32)],
        compiler_params=_cparams(("arbitrary", "arbitrary", "arbitrary")),
        name="fox_attention_prompt",
    )(q, kb, vt, cx)


def _attn_sample_kernel(q_ref, kc_ref, vc_ref, kn_ref, vn_ref, cc_ref, cn_ref, o_ref):
    t = q_ref.shape[1]
    p = kc_ref.shape[1]
    sel = _head_select()
    kc_all = kc_ref[0].reshape(p, D_B).astype(BF16)
    vc_all = vc_ref[0].reshape(p, D_B).astype(BF16)
    row = lax.broadcasted_iota(I32, (t, t), 0)
    col = lax.broadcasted_iota(I32, (t, t), 1)
    for hp in range(N_HEAD_BLOCKS):
        lanes = slice(hp * LANES, (hp + 1) * LANES)
        q2 = q_ref[0, :, lanes]
        zero = jnp.zeros_like(q2)
        kc = kc_all[:, lanes]
        vc = vc_all[:, lanes]
        kn = kn_ref[0, :, lanes]
        vn = vn_ref[0, :, lanes]
        outs = []
        for h in range(HEADS_PER_BLOCK):
            qh = jnp.where(sel[h], q2, zero)
            cbase = cn_ref[0, hp, h:h + 1, 0:1]
            s_c = _scores(qh, kc) + (cbase - cc_ref[0, hp, h:h + 1, :])
            s_n = _scores(qh, kn) + (cbase - cn_ref[0, hp, h:h + 1, :])
            s_n = jnp.where(col <= row, s_n, -jnp.inf)
            m = jnp.maximum(jnp.max(s_c, axis=-1, keepdims=True), jnp.max(s_n, axis=-1, keepdims=True))
            p_c = jnp.exp(s_c - m)
            p_n = jnp.exp(s_n - m)
            l = jnp.sum(p_c, axis=-1, keepdims=True) + jnp.sum(p_n, axis=-1, keepdims=True)
            o = (jnp.dot(p_c.astype(BF16), vc, preferred_element_type=F32)
                 + jnp.dot(p_n.astype(BF16), vn, preferred_element_type=F32))
            outs.append(o / l)
        o_ref[0, :, lanes] = jnp.where(sel[0], outs[0], outs[1])


def _attention_sample(q, cache_k, cache_v, kb, vb, c_cache, c_new):
    b, t, _ = q.shape
    p = cache_k.shape[1]
    newblk = pl.BlockSpec((1, t, D_B), lambda bi: (bi, 0, 0))
    cacheblk = pl.BlockSpec((1, p, N_B_HEADS, HEAD_DIM), lambda bi: (bi, 0, 0, 0))
    return pl.pallas_call(
        _attn_sample_kernel,
        grid=(b,),
        in_specs=[newblk, cacheblk, cacheblk, newblk, newblk,
                  pl.BlockSpec((1, N_HEAD_BLOCKS, HEADS_PER_BLOCK, p), lambda bi: (bi, 0, 0, 0)),
                  pl.BlockSpec((1, N_HEAD_BLOCKS, HEADS_PER_BLOCK, t), lambda bi: (bi, 0, 0, 0))],
        out_specs=newblk,
        out_shape=jax.ShapeDtypeStruct((b, t, D_B), F32),
        compiler_params=_cparams(("arbitrary",), vmem=56 * 1024 * 1024),
        name="fox_attention_sample",
    )(q, cache_k, cache_v, kb, vb, c_cache, c_new)


def _tail_kernel(oap_ref, obp_ref, xp_ref, oas_ref, obs_ref, xs_ref, noa_ref, nob_ref, wout_ref,
                 nffn_ref, wr_ref, br_ref,
                 x2_ref, h3_ref, tidx_ref, tw_ref, rank_ref, cnt_ref, carry_sc):
    i = pl.program_id(0)
    is_sample = i == pl.num_programs(0) - 1
    oa = jnp.where(is_sample, oas_ref[...], oap_ref[...])
    ob = jnp.where(is_sample, obs_ref[...], obp_ref[...])
    x = jnp.where(is_sample, xs_ref[...], xp_ref[...])

    ma = (_rms(oa) * noa_ref[...]).astype(BF16)
    mb = (_rms(ob) * nob_ref[...]).astype(BF16)
    x2 = x + (jnp.dot(ma, wout_ref[:D_A, :], preferred_element_type=F32)
              + jnp.dot(mb, wout_ref[D_A:, :], preferred_element_type=F32))
    x2_ref[...] = x2
    h = _rms(x2) * nffn_ref[...]
    for j in range(ROW_TILES):
        h3_ref[_lane_tile(j, TM_TOK), :] = h[:, j * LANES:(j + 1) * LANES]

    logits = lax.dot_general(wr_ref[...], h.astype(BF16), (((1,), (1,)), ((), ())),
                             preferred_element_type=F32) + br_ref[...]

    expert_f = lax.broadcasted_iota(I32, (N_EXPERTS, TM_TOK), 0).astype(F32)
    vals, idxs, hots = [], [], []
    work = logits
    for _ in range(TOP_K):
        mx = jnp.max(work, axis=0, keepdims=True)
        idx = jnp.min(jnp.where(work == mx, expert_f, float(N_EXPERTS)), axis=0, keepdims=True)
        hot = expert_f == idx
        vals.append(mx)
        idxs.append(idx.astype(I32))
        hots.append(hot)
        work = jnp.where(hot, -jnp.inf, work)
    exps = [jnp.exp(v - vals[0]) for v in vals]
    denom = exps[0] + exps[1] + exps[2] + exps[3]

    @pl.when(i == 0)
    def _():
        carry_sc[...] = jnp.zeros(carry_sc.shape, F32)

    assigned = jnp.zeros((N_EXPERTS, TM_TOK), F32)
    for hot in hots:
        assigned = assigned + jnp.where(hot, 1.0, 0.0)
    ri = lax.broadcasted_iota(I32, (TM_TOK, TM_TOK), 0)
    ci = lax.broadcasted_iota(I32, (TM_TOK, TM_TOK), 1)
    earlier = jnp.where(ri < ci, 1.0, 0.0).astype(BF16)
    before = jnp.dot(assigned.astype(BF16), earlier, preferred_element_type=F32) + carry_sc[...]

    row = lax.broadcasted_iota(I32, (SUBLANES, TM_TOK), 0)
    tidx = jnp.zeros((SUBLANES, TM_TOK), I32)
    tw = jnp.zeros((SUBLANES, TM_TOK), F32)
    rank = jnp.zeros((SUBLANES, TM_TOK), I32)
    for k in range(TOP_K):
        rk = jnp.sum(jnp.where(hots[k], before, 0.0), axis=0, keepdims=True).astype(I32)
        tidx = jnp.where(row == k, idxs[k], tidx)
        tw = jnp.where(row == k, exps[k] / denom, tw)
        rank = jnp.where(row == k, rk, rank)
    tidx_ref[...] = tidx
    tw_ref[...] = tw
    rank_ref[...] = rank
    total = carry_sc[...] + jnp.sum(assigned, axis=1, keepdims=True)
    carry_sc[...] = total
    cnt_ref[...] = jnp.broadcast_to(total, cnt_ref.shape)


def _tail(oa_p, ob_p, x_p, oa_s, ob_s, x_s, noa, nob, wout_b, nffn, wr_b, br_pad):
    n_prompt_tiles = x_p.shape[0] // TM_TOK
    n_tiles = n_prompt_tiles + 1
    n_pad = n_tiles * TM_TOK
    pblk = lambda w: pl.BlockSpec((TM_TOK, w), lambda i: (jnp.minimum(i, n_prompt_tiles - 1), 0))
    full = lambda shape: pl.BlockSpec(shape, lambda i: (0,) * len(shape))
    oblk = lambda w: pl.BlockSpec((TM_TOK, w), lambda i: (i, 0))
    out_shape = (
        jax.ShapeDtypeStruct((n_pad, D_MODEL), F32),
        jax.ShapeDtypeStruct((n_pad * ROW_TILES, LANES), F32),
        jax.ShapeDtypeStruct((SUBLANES, n_pad), I32),
        jax.ShapeDtypeStruct((SUBLANES, n_pad), F32),
        jax.ShapeDtypeStruct((SUBLANES, n_pad), I32),
        jax.ShapeDtypeStruct((N_EXPERTS, LANES), F32),
    )
    tblk = pl.BlockSpec((SUBLANES, TM_TOK), lambda i: (0, i))
    return pl.pallas_call(
        _tail_kernel,
        grid=(n_tiles,),
        in_specs=[pblk(D_A), pblk(D_B), pblk(D_MODEL), full((TM_TOK, D_A)), full((TM_TOK, D_B)),
                  full((TM_TOK, D_MODEL)), full((1, D_A)), full((1, D_B)), full((D_MODEL, D_MODEL)),
                  full((1, D_MODEL)), full((N_EXPERTS, D_MODEL)), full((N_EXPERTS, 1))],
        out_specs=(oblk(D_MODEL), pl.BlockSpec((TM_TOK * ROW_TILES, LANES), lambda i: (i, 0)),
                   tblk, tblk, tblk, full((N_EXPERTS, LANES))),
        out_shape=out_shape,
        scratch_shapes=[pltpu.VMEM((N_EXPERTS, 1), F32)],
        compiler_params=_cparams(("arbitrary",)),
        name="outproj_router",
    )(oa_p, ob_p, x_p, oa_s, ob_s, x_s, noa, nob, wout_b, nffn, wr_b, br_pad)


def _row_copy(src, src_row, dst, dst_row, sem):
    return pltpu.make_async_copy(src.at[_tile_rows(src_row, 1)], dst.at[_tile_rows(dst_row, 1)], sem)


N_ASSIGN = TM_TOK * TOP_K
SLOT_RING = 3
ISSUE_UNROLL = 4


def _slot_copy(slot_hbm, slot_sm, ssem, tile):
    place = tile % SLOT_RING
    return pltpu.make_async_copy(slot_hbm.at[pl.ds(pl.multiple_of(tile * N_ASSIGN, N_ASSIGN), N_ASSIGN)],
                                 slot_sm.at[pl.ds(pl.multiple_of(place * N_ASSIGN, N_ASSIGN), N_ASSIGN)],
                                 ssem.at[place])


def _for_each_assignment(slot_sm, tile, fn):
    base = (tile % SLOT_RING) * N_ASSIGN

    def body(g, carry):
        first = g * ISSUE_UNROLL
        slots = [slot_sm[base + first * TOP_K + a] for a in range(ISSUE_UNROLL * TOP_K)]
        for u in range(ISSUE_UNROLL):
            for k in range(TOP_K):
                fn(first + u, k, slots[u * TOP_K + k])
        return carry

    lax.fori_loop(0, TM_TOK // ISSUE_UNROLL, body, 0)


def _dispatch_kernel(pad_start_ref, pad_len_ref, nu_ref, slot_hbm, h3_hbm, xs_hbm,
                     slot_sm, stage, zblk, sem, ssem, hsem, zsem):
    i = pl.program_id(0)
    n = pl.num_programs(0)

    def stage_copy(tile):
        place = tile % SLOT_RING
        return pltpu.make_async_copy(h3_hbm.at[_tile_rows(tile * TM_TOK, TM_TOK)],
                                     stage.at[_tile_rows(place * TM_TOK, TM_TOK)], hsem.at[place])

    @pl.when(i == 0)
    def _():
        _slot_copy(slot_hbm, slot_sm, ssem, 0).start()
        stage_copy(0).start()

    @pl.when(i + 1 < n)
    def _():
        _slot_copy(slot_hbm, slot_sm, ssem, i + 1).start()
        stage_copy(i + 1).start()

    _slot_copy(slot_hbm, slot_sm, ssem, i).wait()
    stage_copy(i).wait()

    def row_copy(tile, t, slot):
        return _row_copy(stage, (tile % SLOT_RING) * TM_TOK + t, xs_hbm, slot, sem.at[tile % 2])

    _for_each_assignment(slot_sm, i, lambda t, k, slot: row_copy(i, t, slot).start(priority=k % 2))

    @pl.when(i > 0)
    def _():
        _for_each_assignment(slot_sm, i - 1, lambda t, k, slot: row_copy(i - 1, t, slot).wait())

    @pl.when(i == n - 1)
    def _():
        _for_each_assignment(slot_sm, i, lambda t, k, slot: row_copy(i, t, slot).wait())
        zblk[...] = jnp.zeros(zblk.shape, F32)
        for e in range(N_EXPERTS):
            start = pad_start_ref[e]
            n_pad_rows = pad_len_ref[e]

            def zissue(r, carry):
                _row_copy(zblk, 0, xs_hbm, start + r, zsem).start()
                return carry

            def zdrain(r, carry):
                _row_copy(zblk, 0, xs_hbm, start + r, zsem).wait()
                return carry

            lax.fori_loop(0, n_pad_rows, zissue, 0)
            lax.fori_loop(0, n_pad_rows, zdrain, 0)

        def tile_copy(r):
            return pltpu.make_async_copy(zblk, xs_hbm.at[_tile_rows(r * TM_MOE, TM_MOE)], zsem)

        def tissue(r, carry):
            tile_copy(r).start()
            return carry

        def tdrain(r, carry):
            tile_copy(r).wait()
            return carry

        n_tiles_moe = xs_hbm.shape[0] // (TM_MOE * ROW_TILES)
        lax.fori_loop(nu_ref[0], n_tiles_moe, tissue, 0)
        lax.fori_loop(nu_ref[0], n_tiles_moe, tdrain, 0)


def _dispatch(pad_start, pad_len, n_used, slot_flat, h3, n_slots):
    n_tiles = h3.shape[0] // (TM_TOK * ROW_TILES)
    grid_spec = pltpu.PrefetchScalarGridSpec(
        num_scalar_prefetch=3,
        grid=(n_tiles,),
        in_specs=[pl.BlockSpec(memory_space=pl.ANY), pl.BlockSpec(memory_space=pl.ANY)],
        out_specs=pl.BlockSpec(memory_space=pl.ANY),
        scratch_shapes=[pltpu.SMEM((SLOT_RING * N_ASSIGN,), I32),
                        pltpu.VMEM((SLOT_RING * TM_TOK * ROW_TILES, LANES), F32),
                        pltpu.VMEM((TM_MOE * ROW_TILES, LANES), F32),
                        pltpu.SemaphoreType.DMA((2,)), pltpu.SemaphoreType.DMA((SLOT_RING,)),
                        pltpu.SemaphoreType.DMA((SLOT_RING,)), pltpu.SemaphoreType.DMA],
    )
    return pl.pallas_call(
        _dispatch_kernel,
        grid_spec=grid_spec,
        out_shape=jax.ShapeDtypeStruct((n_slots * ROW_TILES, LANES), F32),
        compiler_params=_cparams(("arbitrary",)),
        name="moe_dispatch",
    )(pad_start, pad_len, n_used, slot_flat, h3)


def _moe_kernel(te_ref, nu_ref, xs_ref, w1_ref, b1_ref, w2_ref, b2_ref, ys_ref, w1b_sc, w2b_sc):
    i = pl.program_id(0)
    prev = te_ref[jnp.maximum(i - 1, 0)]
    new_expert = jnp.logical_or(i == 0, te_ref[i] != prev)

    @pl.when(new_expert)
    def _():
        blk = 128
        for r in range(D_MODEL // blk):
            w1b_sc[r * blk:(r + 1) * blk, :] = w1_ref[0, r * blk:(r + 1) * blk, :].astype(BF16)
        for r in range(D_FF // blk):
            w2b_sc[r * blk:(r + 1) * blk, :] = w2_ref[0, r * blk:(r + 1) * blk, :].astype(BF16)

    @pl.when(i < nu_ref[0])
    def _():
        xb = jnp.concatenate([xs_ref[_lane_tile(j, TM_MOE), :] for j in range(ROW_TILES)], axis=-1).astype(BF16)
        acc = jnp.zeros((TM_MOE, D_MODEL), F32)
        for c in range(D_FF // FF_BLK):
            lo = c * FF_BLK
            glu = jnp.dot(xb, w1b_sc[:, lo:lo + FF_BLK], preferred_element_type=F32) + b1_ref[0, :, lo:lo + FF_BLK]
            lin = (jnp.dot(xb, w1b_sc[:, D_FF + lo:D_FF + lo + FF_BLK], preferred_element_type=F32)
                   + b1_ref[0, :, D_FF + lo:D_FF + lo + FF_BLK])
            glu = jnp.minimum(glu, SWIGLU_LIMIT)
            lin = jnp.clip(lin, -SWIGLU_LIMIT, SWIGLU_LIMIT)
            act = glu * _sigmoid(SWIGLU_ALPHA * glu) * (lin + 1.0)
            acc = acc + jnp.dot(act.astype(BF16), w2b_sc[lo:lo + FF_BLK, :], preferred_element_type=F32)
        out = acc + b2_ref[0]
        for j in range(ROW_TILES):
            ys_ref[_lane_tile(j, TM_MOE), :] = out[:, j * LANES:(j + 1) * LANES]

    @pl.when(i >= nu_ref[0])
    def _():
        ys_ref[...] = jnp.zeros(ys_ref.shape, F32)


def _moe(tile_expert, n_used, xs, w1, b1, w2, b2):
    n_tiles = xs.shape[0] // (TM_MOE * ROW_TILES)
    exp_map = lambda i, te, nu: (te[i], 0, 0)
    grid_spec = pltpu.PrefetchScalarGridSpec(
        num_scalar_prefetch=2,
        grid=(n_tiles,),
        in_specs=[pl.BlockSpec((TM_MOE * ROW_TILES, LANES), lambda i, te, nu: (jnp.minimum(i, nu[0] - 1), 0)),
                  pl.BlockSpec((1, D_MODEL, 2 * D_FF), exp_map),
                  pl.BlockSpec((1, 1, 2 * D_FF), exp_map),
                  pl.BlockSpec((1, D_FF, D_MODEL), exp_map),
                  pl.BlockSpec((1, 1, D_MODEL), exp_map)],
        out_specs=pl.BlockSpec((TM_MOE * ROW_TILES, LANES), lambda i, te, nu: (i, 0)),
        scratch_shapes=[pltpu.VMEM((D_MODEL, 2 * D_FF), BF16), pltpu.VMEM((D_FF, D_MODEL), BF16)],
    )
    return pl.pallas_call(
        _moe_kernel,
        grid_spec=grid_spec,
        out_shape=jax.ShapeDtypeStruct(xs.shape, F32),
        compiler_params=_cparams(("arbitrary",), vmem=56 * 1024 * 1024),
        name="moe_experts",
    )(tile_expert, n_used, xs, w1, b1, w2, b2)


def _combine_kernel(slot_hbm, ys_hbm, x2_ref, tw_ref, nf_ref, yp_ref, ys_ref, slot_sm, buf_a, buf_b, sem, ssem):
    i = pl.program_id(0)
    n = pl.num_programs(0)

    def row_copy(tile, buf, t, k, slot):
        return _row_copy(ys_hbm, slot, buf, k * TM_TOK + t, sem.at[tile % 2])

    def start_gather(tile, buf):
        _for_each_assignment(slot_sm, tile, lambda t, k, slot: row_copy(tile, buf, t, k, slot).start(priority=k % 2))

    @pl.when(i == 0)
    def _():
        first = _slot_copy(slot_hbm, slot_sm, ssem, 0)
        first.start()
        first.wait()
        start_gather(0, buf_a)
        _slot_copy(slot_hbm, slot_sm, ssem, 1).start()

    @pl.when(i + 2 < n)
    def _():
        _slot_copy(slot_hbm, slot_sm, ssem, i + 2).start()

    def work(buf, other):
        @pl.when(i + 1 < n)
        def _():
            _slot_copy(slot_hbm, slot_sm, ssem, i + 1).wait()
            start_gather(i + 1, other)

        _for_each_assignment(slot_sm, i, lambda t, k, slot: row_copy(i, buf, t, k, slot).wait())
        tw = tw_ref[...]
        parts = []
        for j in range(ROW_TILES):
            acc = jnp.zeros((TM_TOK, LANES), F32)
            for k in range(TOP_K):
                acc = acc + tw[:, k:k + 1] * buf[_lane_tile(j, TM_TOK, first_row=k * TM_TOK), :]
            parts.append(acc)
        y = _rms(x2_ref[...] + jnp.concatenate(parts, axis=-1)) * nf_ref[...]

        @pl.when(i < n - 1)
        def _():
            yp_ref[...] = y

        @pl.when(i == n - 1)
        def _():
            ys_ref[...] = y

    @pl.when(i % 2 == 0)
    def _():
        work(buf_a, buf_b)

    @pl.when(i % 2 == 1)
    def _():
        work(buf_b, buf_a)


def _combine(slot_flat, ys, x2, tw, norm_final):
    n_tiles = x2.shape[0] // TM_TOK
    n_prompt_tiles = n_tiles - 1
    return pl.pallas_call(
        _combine_kernel,
        grid=(n_tiles,),
        in_specs=[pl.BlockSpec(memory_space=pl.ANY), pl.BlockSpec(memory_space=pl.ANY),
                  pl.BlockSpec((TM_TOK, D_MODEL), lambda i: (i, 0)),
                  pl.BlockSpec((TM_TOK, TOP_K), lambda i: (i, 0)),
                  pl.BlockSpec((1, D_MODEL), lambda i: (0, 0))],
        out_specs=(pl.BlockSpec((TM_TOK, D_MODEL), lambda i: (jnp.minimum(i, n_prompt_tiles - 1), 0)),
                   pl.BlockSpec((TM_TOK, D_MODEL), lambda i: (0, 0))),
        out_shape=(jax.ShapeDtypeStruct((n_prompt_tiles * TM_TOK, D_MODEL), F32),
                   jax.ShapeDtypeStruct((TM_TOK, D_MODEL), F32)),
        scratch_shapes=[pltpu.SMEM((SLOT_RING * N_ASSIGN,), I32),
                        pltpu.VMEM((TOP_K * TM_TOK * ROW_TILES, LANES), F32),
                        pltpu.VMEM((TOP_K * TM_TOK * ROW_TILES, LANES), F32),
                        pltpu.SemaphoreType.DMA((2,)), pltpu.SemaphoreType.DMA((SLOT_RING,))],
        compiler_params=_cparams(("arbitrary",)),
        name="moe_combine",
    )(slot_flat, ys, x2, tw, norm_final)


def _head_pairs(c_bht):
    b, h, t = c_bht.shape
    return c_bht.reshape(b, N_HEAD_BLOCKS, HEADS_PER_BLOCK, t)


def _bias_lanes(pieces_bht):
    b, h, t = pieces_bht[0].shape
    x = jnp.stack(pieces_bht, axis=-1).astype(BF16)
    x = x.reshape(b, N_HEAD_BLOCKS, HEADS_PER_BLOCK, t, BIAS_PIECES).transpose(0, 1, 3, 2, 4)
    x = x.reshape(b, N_HEAD_BLOCKS, t, HEADS_PER_BLOCK * BIAS_PIECES)
    return jnp.pad(x, ((0, 0), (0, 0), (0, 0), (0, LANES - HEADS_PER_BLOCK * BIAS_PIECES)))


def _routing_tables(tidx, rank, counts, n_tiles_moe):
    padded = ((counts + TM_MOE - 1) // TM_MOE) * TM_MOE
    ends = jnp.cumsum(padded)
    offs = ends - padded
    experts = jnp.arange(N_EXPERTS, dtype=I32)
    off_of = jnp.sum(jnp.where(tidx[..., None] == experts, offs, 0), axis=-1)
    slot = (off_of + rank).T.reshape(-1).astype(I32)
    n_used = (ends[-1] // TM_MOE).astype(I32)
    tile_start = jnp.arange(n_tiles_moe, dtype=I32) * TM_MOE
    te = jnp.sum((tile_start[:, None] >= ends[None, :]).astype(I32), axis=-1)
    te_last = jnp.sum(((n_used - 1) * TM_MOE >= ends).astype(I32))
    te = jnp.minimum(jnp.where(jnp.arange(n_tiles_moe) < n_used, te, te_last), N_EXPERTS - 1).astype(I32)
    return slot, te, n_used.reshape(1), (offs + counts).astype(I32), (padded - counts).astype(I32)


def kernel(x_prompt, x_sample, cache_k, cache_v, cache_logf, norm_mix, w_in, b_f, norm_av, w_s, b_s,
           norm_out_a, norm_out_b, w_out, norm_ffn, w_router, b_router, w1, b1, w2, b2, norm_final):
    depth = w_in.shape[0]
    bsz, seq, _ = x_prompt.shape
    dbsz, dseq, _ = x_sample.shape
    past = cache_k.shape[2]
    n_prompt = bsz * seq
    n_sample = dbsz * dseq
    assert depth == 1, "the combine kernel applies the final norm, so it must follow the only layer"
    assert seq % TM_PROJ == 0 and n_prompt % TM_TOK == 0 and seq % TQ == 0
    assert n_sample == MLP_CHUNK and MLP_CHUNK % dseq == 0 and n_sample <= TM_TOK
    n_pad = n_prompt + TM_TOK
    n_tiles_moe = (n_pad * TOP_K) // TM_MOE + N_EXPERTS
    n_slots = n_tiles_moe * TM_MOE

    xp = x_prompt.reshape(n_prompt, D_MODEL)
    xs = x_sample.reshape(n_sample, D_MODEL)
    outs = {k: [] for k in ("kp", "vp", "lfp", "ks", "vs", "lfs", "gvs")}
    for l in range(depth):
        win_b = jnp.pad(w_in[l], ((0, 0), (0, W_IN_COLS - w_in.shape[2]))).astype(BF16)
        bf_pad = jnp.pad(b_f[l], (0, LANES - N_B_HEADS)).reshape(1, LANES)
        nm = norm_mix[l].reshape(1, D_MODEL)
        nav = norm_av[l].reshape(1, D_A)
        reps = MLP_CHUNK // dseq
        ws_s = jnp.tile(w_s[l][:, :dseq, :dseq], (1, reps, reps))
        bst_p = b_s[l].T
        bst_s = jnp.tile(b_s[l][:, :dseq].T, (reps, 1))

        oa_p, q_p, k_p, v_p, kb_p, vt_p, lf_p, _ = _in_projection(
            xp, nm, win_b, bf_pad, nav, w_s[l], bst_p, tm=TM_PROJ, period=MLP_CHUNK, seq_for_vt=seq)
        lf_p3 = lf_p.reshape(bsz, seq, N_B_HEADS)
        _, *neg_c_pieces = _cumsum_time(lf_p3.transpose(0, 2, 1))
        ob_p = _attention_prompt(q_p.reshape(bsz, seq, D_B), kb_p.reshape(bsz, seq, D_B),
                                 vt_p, _bias_lanes(neg_c_pieces))

        oa_s, q_s, k_s, v_s, kb_s, vb_s, lf_s, vn_s = _in_projection(
            xs, nm, win_b, bf_pad, nav, ws_s, bst_s, tm=n_sample, period=dseq)
        lf_s3 = lf_s.reshape(dbsz, dseq, N_B_HEADS)
        t_all = past + dseq
        t_padded = -(-t_all // CS_BLK) * CS_BLK
        lf_all = jnp.concatenate([cache_logf[l].astype(F32), lf_s3], axis=1).transpose(0, 2, 1)
        c_all = _cumsum_time(jnp.pad(lf_all, ((0, 0), (0, 0), (0, t_padded - t_all))))[0]
        ob_s = _attention_sample(q_s.reshape(dbsz, dseq, D_B),
                                 cache_k[l], cache_v[l],
                                 kb_s.reshape(dbsz, dseq, D_B), vb_s.reshape(dbsz, dseq, D_B),
                                 _head_pairs(c_all[:, :, :past]), _head_pairs(c_all[:, :, past:t_all]))

        pad_rows = lambda a: jnp.pad(a, ((0, TM_TOK - n_sample), (0, 0)))
        wr_b = w_router[l].T.astype(BF16)
        br_pad = b_router[l].reshape(N_EXPERTS, 1)
        x2, h3, tidx, tw, rank, cnt = _tail(
            oa_p, ob_p.reshape(n_prompt, D_B), xp, pad_rows(oa_s), pad_rows(ob_s.reshape(n_sample, D_B)),
            pad_rows(xs), norm_out_a[l].reshape(1, D_A), norm_out_b[l].reshape(1, D_B), w_out[l].astype(BF16),
            norm_ffn[l].reshape(1, D_MODEL), wr_b, br_pad)
        counts = cnt[:, 0].astype(I32)
        slot, te, n_used, pad_start, pad_len = _routing_tables(tidx[:TOP_K], rank[:TOP_K], counts, n_tiles_moe)
        tw = tw[:TOP_K].T
        xsort = _dispatch(pad_start, pad_len, n_used, slot, h3, n_slots)
        ysort = _moe(te, n_used, xsort, w1[l], b1[l].reshape(N_EXPERTS, 1, 2 * D_FF), w2[l],
                     b2[l].reshape(N_EXPERTS, 1, D_MODEL))
        xp, y_tail = _combine(slot, ysort, x2, tw, norm_final.reshape(1, D_MODEL))
        xs = y_tail[:n_sample]

        outs["kp"].append(k_p.reshape(bsz, seq, N_B_HEADS, HEAD_DIM))
        outs["vp"].append(v_p.reshape(bsz, seq, N_B_HEADS, HEAD_DIM))
        outs["lfp"].append(lf_p3)
        outs["ks"].append(k_s.reshape(dbsz, dseq, N_B_HEADS, HEAD_DIM))
        outs["vs"].append(v_s.reshape(dbsz, dseq, N_B_HEADS, HEAD_DIM))
        outs["lfs"].append(lf_s3)
        outs["gvs"].append(vn_s.reshape(dbsz, dseq, D_A))

    y_prompt = xp.reshape(bsz, seq, D_MODEL)
    y_sample = xs.reshape(dbsz, dseq, D_MODEL)
    st = jnp.stack
    return (y_prompt, y_sample, st(outs["kp"]), st(outs["vp"]), st(outs["lfp"]),
            st(outs["ks"]), st(outs["vs"]), st(outs["lfs"]), st(outs["gvs"]))
```

```python
import functools

import jax
import jax.numpy as jnp
import numpy as np
from jax import lax
from jax.experimental import pallas as pl
from jax.experimental.pallas import tpu as pltpu

F32 = jnp.float32
BF16 = jnp.bfloat16
I32 = jnp.int32

D_MODEL = 1024
CHUNK = 64
MLP_CHUNK = 128
D_A = 512
A_GROUPS = 4
A_GROUP_CH = D_A // A_GROUPS
D_B = 512
N_B_HEADS = 8
HEAD_DIM = D_B // N_B_HEADS
N_EXPERTS = 32
TOP_K = 4
D_FF = D_MODEL
SWIGLU_ALPHA = 1.702
SWIGLU_LIMIT = 7.0
RMS_EPS = 1e-6

LANES = 128
SUBLANES = 8
ROW_TILES = D_MODEL // LANES
HEADS_PER_BLOCK = LANES // HEAD_DIM
N_HEAD_BLOCKS = N_B_HEADS // HEADS_PER_BLOCK
QKV_OFF = 2 * D_A
F_OFF = 2 * D_A + 3 * D_B
W_IN_COLS = F_OFF + LANES

TM_PROJ = 512
TM_TOK = 256
TM_MOE = 256
TQ = 256
TKB = 1024
KCH = 512
BIAS_PIECES = 3
CS_BLK = 256
FF_BLK = 512
VMEM_LIMIT = 48 * 1024 * 1024


def _cparams(sem, vmem=VMEM_LIMIT):
    return pltpu.CompilerParams(dimension_semantics=sem, vmem_limit_bytes=vmem)


def _gelu(x):
    return 0.5 * x * (1.0 + lax.erf(x * 0.7071067811865476))


def _log_sigmoid(x):
    return jnp.minimum(x, 0.0) - jnp.log1p(jnp.exp(-jnp.abs(x)))


def _sigmoid(x):
    return 1.0 / (1.0 + jnp.exp(-x))


def _rms(x):
    return x * lax.rsqrt(jnp.mean(x * x, axis=-1, keepdims=True) + RMS_EPS)


def _tile_rows(first_row, n_rows):
    return pl.ds(pl.multiple_of(first_row * ROW_TILES, ROW_TILES), n_rows * ROW_TILES)


def _lane_tile(j, n_rows, first_row=0):
    return pl.ds(first_row * ROW_TILES + j, n_rows, stride=ROW_TILES)


def _div_pow2(x, d):
    assert d & (d - 1) == 0
    return jnp.right_shift(x, d.bit_length() - 1)


def _mod_pow2(x, d):
    assert d & (d - 1) == 0
    return jnp.bitwise_and(x, d - 1)


def _store_heads(ref, x):
    ref[...] = x.reshape(x.shape[0], N_B_HEADS, HEAD_DIM)


def _inproj_kernel(x_ref, nm_ref, win_ref, bf_ref, nav_ref, ws_ref, bst_ref,
                   outa_ref, q_ref, k_ref, v_ref, kb_ref, vb_ref, logf_ref, vn_ref, *, period, v_transposed):
    tm = x_ref.shape[0]
    hn = (_rms(x_ref[...]) * nm_ref[...]).astype(BF16)

    def proj(lo, width):
        return jnp.dot(hn, win_ref[:, lo:lo + width], preferred_element_type=F32)

    u = _gelu(proj(0, D_A))
    vn = _rms(_gelu(proj(D_A, D_A))) * nav_ref[...]
    vn_ref[...] = vn
    q_ref[...] = (proj(QKV_OFF, D_B) * (HEAD_DIM ** -0.5)).astype(BF16)
    kk = proj(QKV_OFF + D_B, D_B)
    _store_heads(k_ref, kk)
    kb_ref[...] = kk.astype(BF16)
    vv = proj(QKV_OFF + 2 * D_B, D_B)
    _store_heads(v_ref, vv)
    if v_transposed:
        vb_ref[0] = vv.T.astype(BF16)
    else:
        vb_ref[...] = vv.astype(BF16)
    fl = proj(F_OFF, LANES) + bf_ref[...]
    logf_ref[...] = _log_sigmoid(fl)[:, :N_B_HEADS]

    vnb = vn.astype(BF16)
    ri = lax.broadcasted_iota(I32, (MLP_CHUNK, MLP_CHUNK), 0)
    ci = lax.broadcasted_iota(I32, (MLP_CHUNK, MLP_CHUNK), 1)
    same_stream = _div_pow2(ri, period) == _div_pow2(ci, period)
    visible = _div_pow2(_mod_pow2(ci, period), CHUNK) <= _div_pow2(_mod_pow2(ri, period), CHUNK)
    mask = jnp.logical_and(same_stream, visible)
    for g in range(A_GROUPS):
        w = jnp.where(mask, ws_ref[g], 0.0).astype(BF16)
        bcol = bst_ref[:, g:g + 1]
        cols = slice(g * A_GROUP_CH, (g + 1) * A_GROUP_CH)
        for c in range(tm // MLP_CHUNK):
            rows = slice(c * MLP_CHUNK, (c + 1) * MLP_CHUNK)
            mixed = jnp.dot(w, vnb[rows, cols], preferred_element_type=F32) + bcol
            outa_ref[rows, cols] = u[rows, cols] * mixed


def _in_projection(x2d, norm_mix, win_b, bf_pad, norm_av, ws_tiled, bst_tiled, *, tm, period, seq_for_vt=None):
    rows = x2d.shape[0]
    row_blk = lambda w: pl.BlockSpec((tm, w), lambda i: (i, 0))
    full = lambda shape: pl.BlockSpec(shape, lambda i: (0,) * len(shape))
    head_blk = pl.BlockSpec((tm, N_B_HEADS, HEAD_DIM), lambda i: (i, 0, 0))
    if seq_for_vt is None:
        vb_shape, vb_blk = jax.ShapeDtypeStruct((rows, D_B), BF16), row_blk(D_B)
    else:
        tiles = seq_for_vt // tm
        vb_shape = jax.ShapeDtypeStruct((rows // seq_for_vt, D_B, seq_for_vt), BF16)
        vb_blk = pl.BlockSpec((1, D_B, tm), lambda i: (i // tiles, 0, i % tiles))
    out_shape = (
        jax.ShapeDtypeStruct((rows, D_A), F32),
        jax.ShapeDtypeStruct((rows, D_B), BF16),
        jax.ShapeDtypeStruct((rows, N_B_HEADS, HEAD_DIM), F32),
        jax.ShapeDtypeStruct((rows, N_B_HEADS, HEAD_DIM), F32),
        jax.ShapeDtypeStruct((rows, D_B), BF16),
        vb_shape,
        jax.ShapeDtypeStruct((rows, N_B_HEADS), F32),
        jax.ShapeDtypeStruct((rows, D_A), F32),
    )
    return pl.pallas_call(
        functools.partial(_inproj_kernel, period=period, v_transposed=seq_for_vt is not None),
        grid=(rows // tm,),
        in_specs=[row_blk(D_MODEL), full((1, D_MODEL)), full((D_MODEL, W_IN_COLS)), full((1, LANES)),
                  full((1, D_A)), full((A_GROUPS, MLP_CHUNK, MLP_CHUNK)), full((MLP_CHUNK, A_GROUPS))],
        out_specs=(row_blk(D_A), row_blk(D_B), head_blk, head_blk, row_blk(D_B), vb_blk,
                   row_blk(N_B_HEADS), row_blk(D_A)),
        out_shape=out_shape,
        compiler_params=_cparams(("arbitrary",)),
        name="in_projection",
    )(x2d, norm_mix, win_b, bf_pad, norm_av, ws_tiled, bst_tiled)


def _split_bf16(x):
    x1 = x.astype(BF16)
    r1 = x - x1.astype(F32)
    x2 = r1.astype(BF16)
    x3 = (r1 - x2.astype(F32)).astype(BF16)
    return x1, x2, x3


def _cumsum_kernel(x_ref, o_ref, *piece_refs):
    t = x_ref.shape[2]
    ri = lax.broadcasted_iota(I32, (CS_BLK, CS_BLK), 0)
    ci = lax.broadcasted_iota(I32, (CS_BLK, CS_BLK), 1)
    upper = jnp.where(ri <= ci, 1.0, 0.0).astype(BF16)
    carry = jnp.zeros((N_B_HEADS, 1), F32)
    for c in range(t // CS_BLK):
        cols = slice(c * CS_BLK, (c + 1) * CS_BLK)
        s = sum(jnp.dot(piece, upper, preferred_element_type=F32) for piece in _split_bf16(x_ref[0, :, cols])) + carry
        o_ref[0, :, cols] = s
        for ref, piece in zip(piece_refs, _split_bf16(-s)):
            ref[0, :, cols] = piece.astype(F32)
        carry = s[:, CS_BLK - 1:CS_BLK]


def _cumsum_time(x_bht):
    b, h, t = x_bht.shape
    blk = pl.BlockSpec((1, h, t), lambda i: (i, 0, 0))
    return pl.pallas_call(
        _cumsum_kernel,
        grid=(b,),
        in_specs=[blk],
        out_specs=(blk,) * (1 + BIAS_PIECES),
        out_shape=(jax.ShapeDtypeStruct((b, h, t), F32),) * (1 + BIAS_PIECES),
        compiler_params=_cparams(("arbitrary",)),
        name="cumsum_logf",
    )(x_bht)


def _head_select(width=LANES):
    lane = lax.broadcasted_iota(I32, (1, width), 1)
    return [jnp.logical_and(lane >= h * HEAD_DIM, lane < (h + 1) * HEAD_DIM) for h in range(HEADS_PER_BLOCK)]


def _scores(qh, k2):
    return lax.dot_general(qh, k2, (((1,), (1,)), ((), ())), preferred_element_type=F32)


def _attn_prompt_kernel(q_ref, k_ref, vt_ref, cx_ref, o_ref, m_sc, l_sc, acc_sc, sa_sc, sb_sc):
    i = pl.program_id(2)
    lane = lax.broadcasted_iota(I32, (1, LANES), 1)
    qaug = []
    for hq in range(EXP_HEADS):
        hb, h = hq // HEADS_PER_BLOCK, hq % HEADS_PER_BLOCK
        q2 = q_ref[0, :, hb * LANES:(hb + 1) * LANES]
        head = jnp.logical_and(lane >= h * HEAD_DIM, lane < (h + 1) * HEAD_DIM)
        pick = jnp.logical_and(lane >= h * BIAS_PIECES, lane < (h + 1) * BIAS_PIECES)
        head_row = jnp.where(head, 1.0, 0.0).astype(BF16)
        pick_row = jnp.where(pick, 1.0, 0.0).astype(BF16)
        qaug.append(jnp.concatenate([q2 * head_row, jnp.broadcast_to(pick_row, q2.shape)], axis=1))

    m_sc[...] = jnp.full(m_sc.shape, -jnp.inf, F32)
    l_sc[...] = jnp.zeros(l_sc.shape, F32)
    acc_sc[...] = jnp.zeros(acc_sc.shape, F32)

    def scores_into(c, s_sc):
        kc = pl.multiple_of(c * TQ, TQ)
        for hq in range(EXP_HEADS):
            hb = hq // HEADS_PER_BLOCK
            kk = jnp.concatenate([k_ref[0, pl.ds(kc, TQ), hb * LANES:(hb + 1) * LANES],
                                  cx_ref[0, hb, pl.ds(kc, TQ), :]], axis=1)
            s_sc[hq] = lax.dot_general(kk, qaug[hq], (((1,), (1,)), ((), ())), preferred_element_type=F32)

    def consume(c, s_sc, masked):
        kc = pl.multiple_of(c * TQ, TQ)
        for h in range(EXP_HEADS):
            st = s_sc[h]
            if masked:
                key = lax.broadcasted_iota(I32, (TQ, TQ), 0)
                qry = lax.broadcasted_iota(I32, (TQ, TQ), 1)
                st = jnp.where(key <= qry, st, -jnp.inf)
            m_prev = m_sc[h]
            m_new = jnp.maximum(m_prev, jnp.max(st, axis=0, keepdims=True))
            a = jnp.exp(m_prev - m_new)
            p = jnp.exp(st - m_new)
            l_sc[h] = a * l_sc[h] + jnp.sum(p, axis=0, keepdims=True)
            m_sc[h] = m_new
            vt = vt_ref[0, h * HEAD_DIM:(h + 1) * HEAD_DIM, pl.ds(kc, TQ)]
            acc_sc[h] = a * acc_sc[h] + jnp.dot(vt, p.astype(BF16), preferred_element_type=F32)

    scores_into(0, sa_sc)

    def pair(j, carry):
        scores_into(2 * j + 1, sb_sc)
        consume(2 * j, sa_sc, False)
        scores_into(2 * j + 2, sa_sc)
        consume(2 * j + 1, sb_sc, False)
        return carry

    lax.fori_loop(0, i // 2, pair, 0)

    @pl.when(i % 2 == 0)
    def _():
        consume(i, sa_sc, True)

    @pl.when(i % 2 == 1)
    def _():
        scores_into(i, sb_sc)
        consume(i - 1, sa_sc, False)
        consume(i, sb_sc, True)

    out_t = jnp.concatenate([acc_sc[h] / l_sc[h] for h in range(EXP_HEADS)], axis=0)
    o_ref[0] = out_t.T


EXP_HEADS = 4
EXP_W = EXP_HEADS * HEAD_DIM


def _attention_prompt(q, kb, vt, cx):
    b, s, _ = q.shape
    nq = s // TQ
    return pl.pallas_call(
        _attn_prompt_kernel,
        grid=(b, D_B // EXP_W, nq),
        in_specs=[pl.BlockSpec((1, TQ, EXP_W), lambda bi, hp, i: (bi, i, hp)),
                  pl.BlockSpec((1, s, EXP_W), lambda bi, hp, i: (bi, 0, hp)),
                  pl.BlockSpec((1, EXP_W, s), lambda bi, hp, i: (bi, hp, 0)),
                  pl.BlockSpec((1, EXP_W // LANES, s, LANES), lambda bi, hp, i: (bi, hp, 0, 0))],
        out_specs=pl.BlockSpec((1, TQ, EXP_W), lambda bi, hp, i: (bi, i, hp)),
        out_shape=jax.ShapeDtypeStruct((b, s, D_B), F32),
        scratch_shapes=[pltpu.VMEM((EXP_HEADS, 1, TQ), F32),
                        pltpu.VMEM((EXP_HEADS, 1, TQ), F32),
                        pltpu.VMEM((EXP_HEADS, HEAD_DIM, TQ), F32),
                        pltpu.VMEM((EXP_HEADS, TQ, TQ), F32),
                        pltpu.VMEM((EXP_HEADS, TQ, TQ), F32)],
        compiler_params=_cparams(("arbitrary", "arbitrary", "arbitrary")),
        name="fox_attention_prompt",
    )(q, kb, vt, cx)


def _attn_sample_kernel(q_ref, kc_ref, vc_ref, kn_ref, vn_ref, cc_ref, cn_ref, o_ref):
    t = q_ref.shape[1]
    p = kc_ref.shape[1]
    sel = _head_select()
    kc_all = kc_ref[0].reshape(p, D_B).astype(BF16)
    vc_all = vc_ref[0].reshape(p, D_B).astype(BF16)
    row = lax.broadcasted_iota(I32, (t, t), 0)
    col = lax.broadcasted_iota(I32, (t, t), 1)
    for hp in range(N_HEAD_BLOCKS):
        lanes = slice(hp * LANES, (hp + 1) * LANES)
        q2 = q_ref[0, :, lanes]
        zero = jnp.zeros_like(q2)
        kc = kc_all[:, lanes]
        vc = vc_all[:, lanes]
        kn = kn_ref[0, :, lanes]
        vn = vn_ref[0, :, lanes]
        outs = []
        for h in range(HEADS_PER_BLOCK):
            qh = jnp.where(sel[h], q2, zero)
            cbase = cn_ref[0, hp, h:h + 1, 0:1]
            s_c = _scores(qh, kc) + (cbase - cc_ref[0, hp, h:h + 1, :])
            s_n = _scores(qh, kn) + (cbase - cn_ref[0, hp, h:h + 1, :])
            s_n = jnp.where(col <= row, s_n, -jnp.inf)
            m = jnp.maximum(jnp.max(s_c, axis=-1, keepdims=True), jnp.max(s_n, axis=-1, keepdims=True))
            p_c = jnp.exp(s_c - m)
            p_n = jnp.exp(s_n - m)
            l = jnp.sum(p_c, axis=-1, keepdims=True) + jnp.sum(p_n, axis=-1, keepdims=True)
            o = (jnp.dot(p_c.astype(BF16), vc, preferred_element_type=F32)
                 + jnp.dot(p_n.astype(BF16), vn, preferred_element_type=F32))
            outs.append(o / l)
        o_ref[0, :, lanes] = jnp.where(sel[0], outs[0], outs[1])


def _attention_sample(q, cache_k, cache_v, kb, vb, c_cache, c_new):
    b, t, _ = q.shape
    p = cache_k.shape[1]
    newblk = pl.BlockSpec((1, t, D_B), lambda bi: (bi, 0, 0))
    cacheblk = pl.BlockSpec((1, p, N_B_HEADS, HEAD_DIM), lambda bi: (bi, 0, 0, 0))
    return pl.pallas_call(
        _attn_sample_kernel,
        grid=(b,),
        in_specs=[newblk, cacheblk, cacheblk, newblk, newblk,
                  pl.BlockSpec((1, N_HEAD_BLOCKS, HEADS_PER_BLOCK, p), lambda bi: (bi, 0, 0, 0)),
                  pl.BlockSpec((1, N_HEAD_BLOCKS, HEADS_PER_BLOCK, t), lambda bi: (bi, 0, 0, 0))],
        out_specs=newblk,
        out_shape=jax.ShapeDtypeStruct((b, t, D_B), F32),
        compiler_params=_cparams(("arbitrary",), vmem=56 * 1024 * 1024),
        name="fox_attention_sample",
    )(q, cache_k, cache_v, kb, vb, c_cache, c_new)


def _tail_kernel(oap_ref, obp_ref, xp_ref, oas_ref, obs_ref, xs_ref, noa_ref, nob_ref, wout_ref,
                 nffn_ref, wr_ref, br_ref,
                 x2_ref, h3_ref, tidx_ref, tw_ref, rank_ref, cnt_ref, carry_sc):
    i = pl.program_id(0)
    is_sample = i == pl.num_programs(0) - 1
    oa = jnp.where(is_sample, oas_ref[...], oap_ref[...])
    ob = jnp.where(is_sample, obs_ref[...], obp_ref[...])
    x = jnp.where(is_sample, xs_ref[...], xp_ref[...])

    ma = (_rms(oa) * noa_ref[...]).astype(BF16)
    mb = (_rms(ob) * nob_ref[...]).astype(BF16)
    x2 = x + (jnp.dot(ma, wout_ref[:D_A, :], preferred_element_type=F32)
              + jnp.dot(mb, wout_ref[D_A:, :], preferred_element_type=F32))
    x2_ref[...] = x2
    h = _rms(x2) * nffn_ref[...]
    for j in range(ROW_TILES):
        h3_ref[_lane_tile(j, TM_TOK), :] = h[:, j * LANES:(j + 1) * LANES]

    logits = lax.dot_general(wr_ref[...], h.astype(BF16), (((1,), (1,)), ((), ())),
                             preferred_element_type=F32) + br_ref[...]

    expert_f = lax.broadcasted_iota(I32, (N_EXPERTS, TM_TOK), 0).astype(F32)
    vals, idxs, hots = [], [], []
    work = logits
    for _ in range(TOP_K):
        mx = jnp.max(work, axis=0, keepdims=True)
        idx = jnp.min(jnp.where(work == mx, expert_f, float(N_EXPERTS)), axis=0, keepdims=True)
        hot = expert_f == idx
        vals.append(mx)
        idxs.append(idx.astype(I32))
        hots.append(hot)
        work = jnp.where(hot, -jnp.inf, work)
    exps = [jnp.exp(v - vals[0]) for v in vals]
    denom = exps[0] + exps[1] + exps[2] + exps[3]

    @pl.when(i == 0)
    def _():
        carry_sc[...] = jnp.zeros(carry_sc.shape, F32)

    assigned = jnp.zeros((N_EXPERTS, TM_TOK), F32)
    for hot in hots:
        assigned = assigned + jnp.where(hot, 1.0, 0.0)
    ri = lax.broadcasted_iota(I32, (TM_TOK, TM_TOK), 0)
    ci = lax.broadcasted_iota(I32, (TM_TOK, TM_TOK), 1)
    earlier = jnp.where(ri < ci, 1.0, 0.0).astype(BF16)
    before = jnp.dot(assigned.astype(BF16), earlier, preferred_element_type=F32) + carry_sc[...]

    row = lax.broadcasted_iota(I32, (SUBLANES, TM_TOK), 0)
    tidx = jnp.zeros((SUBLANES, TM_TOK), I32)
    tw = jnp.zeros((SUBLANES, TM_TOK), F32)
    rank = jnp.zeros((SUBLANES, TM_TOK), I32)
    for k in range(TOP_K):
        rk = jnp.sum(jnp.where(hots[k], before, 0.0), axis=0, keepdims=True).astype(I32)
        tidx = jnp.where(row == k, idxs[k], tidx)
        tw = jnp.where(row == k, exps[k] / denom, tw)
        rank = jnp.where(row == k, rk, rank)
    tidx_ref[...] = tidx
    tw_ref[...] = tw
    rank_ref[...] = rank
    total = carry_sc[...] + jnp.sum(assigned, axis=1, keepdims=True)
    carry_sc[...] = total
    cnt_ref[...] = jnp.broadcast_to(total, cnt_ref.shape)


def _tail(oa_p, ob_p, x_p, oa_s, ob_s, x_s, noa, nob, wout_b, nffn, wr_b, br_pad):
    n_prompt_tiles = x_p.shape[0] // TM_TOK
    n_tiles = n_prompt_tiles + 1
    n_pad = n_tiles * TM_TOK
    pblk = lambda w: pl.BlockSpec((TM_TOK, w), lambda i: (jnp.minimum(i, n_prompt_tiles - 1), 0))
    full = lambda shape: pl.BlockSpec(shape, lambda i: (0,) * len(shape))
    oblk = lambda w: pl.BlockSpec((TM_TOK, w), lambda i: (i, 0))
    out_shape = (
        jax.ShapeDtypeStruct((n_pad, D_MODEL), F32),
        jax.ShapeDtypeStruct((n_pad * ROW_TILES, LANES), F32),
        jax.ShapeDtypeStruct((SUBLANES, n_pad), I32),
        jax.ShapeDtypeStruct((SUBLANES, n_pad), F32),
        jax.ShapeDtypeStruct((SUBLANES, n_pad), I32),
        jax.ShapeDtypeStruct((N_EXPERTS, LANES), F32),
    )
    tblk = pl.BlockSpec((SUBLANES, TM_TOK), lambda i: (0, i))
    return pl.pallas_call(
        _tail_kernel,
        grid=(n_tiles,),
        in_specs=[pblk(D_A), pblk(D_B), pblk(D_MODEL), full((TM_TOK, D_A)), full((TM_TOK, D_B)),
                  full((TM_TOK, D_MODEL)), full((1, D_A)), full((1, D_B)), full((D_MODEL, D_MODEL)),
                  full((1, D_MODEL)), full((N_EXPERTS, D_MODEL)), full((N_EXPERTS, 1))],
        out_specs=(oblk(D_MODEL), pl.BlockSpec((TM_TOK * ROW_TILES, LANES), lambda i: (i, 0)),
                   tblk, tblk, tblk, full((N_EXPERTS, LANES))),
        out_shape=out_shape,
        scratch_shapes=[pltpu.VMEM((N_EXPERTS, 1), F32)],
        compiler_params=_cparams(("arbitrary",)),
        name="outproj_router",
    )(oa_p, ob_p, x_p, oa_s, ob_s, x_s, noa, nob, wout_b, nffn, wr_b, br_pad)


def _row_copy(src, src_row, dst, dst_row, sem):
    return pltpu.make_async_copy(src.at[_tile_rows(src_row, 1)], dst.at[_tile_rows(dst_row, 1)], sem)


N_ASSIGN = TM_TOK * TOP_K
SLOT_RING = 3
ISSUE_UNROLL = 4


def _slot_copy(slot_hbm, slot_sm, ssem, tile):
    place = tile % SLOT_RING
    return pltpu.make_async_copy(slot_hbm.at[pl.ds(pl.multiple_of(tile * N_ASSIGN, N_ASSIGN), N_ASSIGN)],
                                 slot_sm.at[pl.ds(pl.multiple_of(place * N_ASSIGN, N_ASSIGN), N_ASSIGN)],
                                 ssem.at[place])


def _for_each_assignment(slot_sm, tile, fn):
    base = (tile % SLOT_RING) * N_ASSIGN

    def body(g, carry):
        first = g * ISSUE_UNROLL
        slots = [slot_sm[base + first * TOP_K + a] for a in range(ISSUE_UNROLL * TOP_K)]
        for u in range(ISSUE_UNROLL):
            for k in range(TOP_K):
                fn(first + u, k, slots[u * TOP_K + k])
        return carry

    lax.fori_loop(0, TM_TOK // ISSUE_UNROLL, body, 0)


def _dispatch_kernel(pad_start_ref, pad_len_ref, nu_ref, slot_hbm, h3_hbm, xs_hbm,
                     slot_sm, stage, zblk, sem, ssem, hsem, zsem):
    i = pl.program_id(0)
    n = pl.num_programs(0)

    def stage_copy(tile):
        place = tile % SLOT_RING
        return pltpu.make_async_copy(h3_hbm.at[_tile_rows(tile * TM_TOK, TM_TOK)],
                                     stage.at[_tile_rows(place * TM_TOK, TM_TOK)], hsem.at[place])

    @pl.when(i == 0)
    def _():
        _slot_copy(slot_hbm, slot_sm, ssem, 0).start()
        stage_copy(0).start()

    @pl.when(i + 1 < n)
    def _():
        _slot_copy(slot_hbm, slot_sm, ssem, i + 1).start()
        stage_copy(i + 1).start()

    _slot_copy(slot_hbm, slot_sm, ssem, i).wait()
    stage_copy(i).wait()

    def row_copy(tile, t, slot):
        return _row_copy(stage, (tile % SLOT_RING) * TM_TOK + t, xs_hbm, slot, sem.at[tile % 2])

    _for_each_assignment(slot_sm, i, lambda t, k, slot: row_copy(i, t, slot).start(priority=k % 2))

    @pl.when(i > 0)
    def _():
        _for_each_assignment(slot_sm, i - 1, lambda t, k, slot: row_copy(i - 1, t, slot).wait())

    @pl.when(i == n - 1)
    def _():
        _for_each_assignment(slot_sm, i, lambda t, k, slot: row_copy(i, t, slot).wait())
        zblk[...] = jnp.zeros(zblk.shape, F32)
        for e in range(N_EXPERTS):
            start = pad_start_ref[e]
            n_pad_rows = pad_len_ref[e]

            def zissue(r, carry):
                _row_copy(zblk, 0, xs_hbm, start + r, zsem).start()
                return carry

            def zdrain(r, carry):
                _row_copy(zblk, 0, xs_hbm, start + r, zsem).wait()
                return carry

            lax.fori_loop(0, n_pad_rows, zissue, 0)
            lax.fori_loop(0, n_pad_rows, zdrain, 0)

        def tile_copy(r):
            return pltpu.make_async_copy(zblk, xs_hbm.at[_tile_rows(r * TM_MOE, TM_MOE)], zsem)

        def tissue(r, carry):
            tile_copy(r).start()
            return carry

        def tdrain(r, carry):
            tile_copy(r).wait()
            return carry

        n_tiles_moe = xs_hbm.shape[0] // (TM_MOE * ROW_TILES)
        lax.fori_loop(nu_ref[0], n_tiles_moe, tissue, 0)
        lax.fori_loop(nu_ref[0], n_tiles_moe, tdrain, 0)


def _dispatch(pad_start, pad_len, n_used, slot_flat, h3, n_slots):
    n_tiles = h3.shape[0] // (TM_TOK * ROW_TILES)
    grid_spec = pltpu.PrefetchScalarGridSpec(
        num_scalar_prefetch=3,
        grid=(n_tiles,),
        in_specs=[pl.BlockSpec(memory_space=pl.ANY), pl.BlockSpec(memory_space=pl.ANY)],
        out_specs=pl.BlockSpec(memory_space=pl.ANY),
        scratch_shapes=[pltpu.SMEM((SLOT_RING * N_ASSIGN,), I32),
                        pltpu.VMEM((SLOT_RING * TM_TOK * ROW_TILES, LANES), F32),
                        pltpu.VMEM((TM_MOE * ROW_TILES, LANES), F32),
                        pltpu.SemaphoreType.DMA((2,)), pltpu.SemaphoreType.DMA((SLOT_RING,)),
                        pltpu.SemaphoreType.DMA((SLOT_RING,)), pltpu.SemaphoreType.DMA],
    )
    return pl.pallas_call(
        _dispatch_kernel,
        grid_spec=grid_spec,
        out_shape=jax.ShapeDtypeStruct((n_slots * ROW_TILES, LANES), F32),
        compiler_params=_cparams(("arbitrary",)),
        name="moe_dispatch",
    )(pad_start, pad_len, n_used, slot_flat, h3)


def _moe_kernel(te_ref, nu_ref, xs_ref, w1_ref, b1_ref, w2_ref, b2_ref, ys_ref, w1b_sc, w2b_sc):
    i = pl.program_id(0)
    prev = te_ref[jnp.maximum(i - 1, 0)]
    new_expert = jnp.logical_or(i == 0, te_ref[i] != prev)

    @pl.when(new_expert)
    def _():
        blk = 128
        for r in range(D_MODEL // blk):
            w1b_sc[r * blk:(r + 1) * blk, :] = w1_ref[0, r * blk:(r + 1) * blk, :].astype(BF16)
        for r in range(D_FF // blk):
            w2b_sc[r * blk:(r + 1) * blk, :] = w2_ref[0, r * blk:(r + 1) * blk, :].astype(BF16)

    @pl.when(i < nu_ref[0])
    def _():
        xb = jnp.concatenate([xs_ref[_lane_tile(j, TM_MOE), :] for j in range(ROW_TILES)], axis=-1).astype(BF16)
        acc = jnp.zeros((TM_MOE, D_MODEL), F32)
        for c in range(D_FF // FF_BLK):
            lo = c * FF_BLK
            glu = jnp.dot(xb, w1b_sc[:, lo:lo + FF_BLK], preferred_element_type=F32) + b1_ref[0, :, lo:lo + FF_BLK]
            lin = (jnp.dot(xb, w1b_sc[:, D_FF + lo:D_FF + lo + FF_BLK], preferred_element_type=F32)
                   + b1_ref[0, :, D_FF + lo:D_FF + lo + FF_BLK])
            glu = jnp.minimum(glu, SWIGLU_LIMIT)
            lin = jnp.clip(lin, -SWIGLU_LIMIT, SWIGLU_LIMIT)
            act = glu * _sigmoid(SWIGLU_ALPHA * glu) * (lin + 1.0)
            acc = acc + jnp.dot(act.astype(BF16), w2b_sc[lo:lo + FF_BLK, :], preferred_element_type=F32)
        out = acc + b2_ref[0]
        for j in range(ROW_TILES):
            ys_ref[_lane_tile(j, TM_MOE), :] = out[:, j * LANES:(j + 1) * LANES]

    @pl.when(i >= nu_ref[0])
    def _():
        ys_ref[...] = jnp.zeros(ys_ref.shape, F32)


def _moe(tile_expert, n_used, xs, w1, b1, w2, b2):
    n_tiles = xs.shape[0] // (TM_MOE * ROW_TILES)
    exp_map = lambda i, te, nu: (te[i], 0, 0)
    grid_spec = pltpu.PrefetchScalarGridSpec(
        num_scalar_prefetch=2,
        grid=(n_tiles,),
        in_specs=[pl.BlockSpec((TM_MOE * ROW_TILES, LANES), lambda i, te, nu: (jnp.minimum(i, nu[0] - 1), 0)),
                  pl.BlockSpec((1, D_MODEL, 2 * D_FF), exp_map),
                  pl.BlockSpec((1, 1, 2 * D_FF), exp_map),
                  pl.BlockSpec((1, D_FF, D_MODEL), exp_map),
                  pl.BlockSpec((1, 1, D_MODEL), exp_map)],
        out_specs=pl.BlockSpec((TM_MOE * ROW_TILES, LANES), lambda i, te, nu: (i, 0)),
        scratch_shapes=[pltpu.VMEM((D_MODEL, 2 * D_FF), BF16), pltpu.VMEM((D_FF, D_MODEL), BF16)],
    )
    return pl.pallas_call(
        _moe_kernel,
        grid_spec=grid_spec,
        out_shape=jax.ShapeDtypeStruct(xs.shape, F32),
        compiler_params=_cparams(("arbitrary",), vmem=56 * 1024 * 1024),
        name="moe_experts",
    )(tile_expert, n_used, xs, w1, b1, w2, b2)


def _combine_kernel(slot_hbm, ys_hbm, x2_ref, tw_ref, nf_ref, yp_ref, ys_ref, slot_sm, buf_a, buf_b, sem, ssem):
    i = pl.program_id(0)
    n = pl.num_programs(0)

    def row_copy(tile, buf, t, k, slot):
        return _row_copy(ys_hbm, slot, buf, k * TM_TOK + t, sem.at[tile % 2])

    def start_gather(tile, buf):
        _for_each_assignment(slot_sm, tile, lambda t, k, slot: row_copy(tile, buf, t, k, slot).start(priority=k % 2))

    @pl.when(i == 0)
    def _():
        first = _slot_copy(slot_hbm, slot_sm, ssem, 0)
        first.start()
        first.wait()
        start_gather(0, buf_a)
        _slot_copy(slot_hbm, slot_sm, ssem, 1).start()

    @pl.when(i + 2 < n)
    def _():
        _slot_copy(slot_hbm, slot_sm, ssem, i + 2).start()

    def work(buf, other):
        @pl.when(i + 1 < n)
        def _():
            _slot_copy(slot_hbm, slot_sm, ssem, i + 1).wait()
            start_gather(i + 1, other)

        _for_each_assignment(slot_sm, i, lambda t, k, slot: row_copy(i, buf, t, k, slot).wait())
        tw = tw_ref[...]
        parts = []
        for j in range(ROW_TILES):
            acc = jnp.zeros((TM_TOK, LANES), F32)
            for k in range(TOP_K):
                acc = acc + tw[:, k:k + 1] * buf[_lane_tile(j, TM_TOK, first_row=k * TM_TOK), :]
            parts.append(acc)
        y = _rms(x2_ref[...] + jnp.concatenate(parts, axis=-1)) * nf_ref[...]

        @pl.when(i < n - 1)
        def _():
            yp_ref[...] = y

        @pl.when(i == n - 1)
        def _():
            ys_ref[...] = y

    @pl.when(i % 2 == 0)
    def _():
        work(buf_a, buf_b)

    @pl.when(i % 2 == 1)
    def _():
        work(buf_b, buf_a)


def _combine(slot_flat, ys, x2, tw, norm_final):
    n_tiles = x2.shape[0] // TM_TOK
    n_prompt_tiles = n_tiles - 1
    return pl.pallas_call(
        _combine_kernel,
        grid=(n_tiles,),
        in_specs=[pl.BlockSpec(memory_space=pl.ANY), pl.BlockSpec(memory_space=pl.ANY),
                  pl.BlockSpec((TM_TOK, D_MODEL), lambda i: (i, 0)),
                  pl.BlockSpec((TM_TOK, TOP_K), lambda i: (i, 0)),
                  pl.BlockSpec((1, D_MODEL), lambda i: (0, 0))],
        out_specs=(pl.BlockSpec((TM_TOK, D_MODEL), lambda i: (jnp.minimum(i, n_prompt_tiles - 1), 0)),
                   pl.BlockSpec((TM_TOK, D_MODEL), lambda i: (0, 0))),
        out_shape=(jax.ShapeDtypeStruct((n_prompt_tiles * TM_TOK, D_MODEL), F32),
                   jax.ShapeDtypeStruct((TM_TOK, D_MODEL), F32)),
        scratch_shapes=[pltpu.SMEM((SLOT_RING * N_ASSIGN,), I32),
                        pltpu.VMEM((TOP_K * TM_TOK * ROW_TILES, LANES), F32),
                        pltpu.VMEM((TOP_K * TM_TOK * ROW_TILES, LANES), F32),
                        pltpu.SemaphoreType.DMA((2,)), pltpu.SemaphoreType.DMA((SLOT_RING,))],
        compiler_params=_cparams(("arbitrary",)),
        name="moe_combine",
    )(slot_flat, ys, x2, tw, norm_final)


def _head_pairs(c_bht):
    b, h, t = c_bht.shape
    return c_bht.reshape(b, N_HEAD_BLOCKS, HEADS_PER_BLOCK, t)


def _bias_lanes(pieces_bht):
    b, h, t = pieces_bht[0].shape
    x = jnp.stack(pieces_bht, axis=-1).astype(BF16)
    x = x.reshape(b, N_HEAD_BLOCKS, HEADS_PER_BLOCK, t, BIAS_PIECES).transpose(0, 1, 3, 2, 4)
    x = x.reshape(b, N_HEAD_BLOCKS, t, HEADS_PER_BLOCK * BIAS_PIECES)
    return jnp.pad(x, ((0, 0), (0, 0), (0, 0), (0, LANES - HEADS_PER_BLOCK * BIAS_PIECES)))


def _routing_tables(tidx, rank, counts, n_tiles_moe):
    padded = ((counts + TM_MOE - 1) // TM_MOE) * TM_MOE
    ends = jnp.cumsum(padded)
    offs = ends - padded
    experts = jnp.arange(N_EXPERTS, dtype=I32)
    off_of = jnp.sum(jnp.where(tidx[..., None] == experts, offs, 0), axis=-1)
    slot = (off_of + rank).T.reshape(-1).astype(I32)
    n_used = (ends[-1] // TM_MOE).astype(I32)
    tile_start = jnp.arange(n_tiles_moe, dtype=I32) * TM_MOE
    te = jnp.sum((tile_start[:, None] >= ends[None, :]).astype(I32), axis=-1)
    te_last = jnp.sum(((n_used - 1) * TM_MOE >= ends).astype(I32))
    te = jnp.minimum(jnp.where(jnp.arange(n_tiles_moe) < n_used, te, te_last), N_EXPERTS - 1).astype(I32)
    return slot, te, n_used.reshape(1), (offs + counts).astype(I32), (padded - counts).astype(I32)


def kernel(x_prompt, x_sample, cache_k, cache_v, cache_logf, norm_mix, w_in, b_f, norm_av, w_s, b_s,
           norm_out_a, norm_out_b, w_out, norm_ffn, w_router, b_router, w1, b1, w2, b2, norm_final):
    depth = w_in.shape[0]
    bsz, seq, _ = x_prompt.shape
    dbsz, dseq, _ = x_sample.shape
    past = cache_k.shape[2]
    n_prompt = bsz * seq
    n_sample = dbsz * dseq
    assert depth == 1, "the combine kernel applies the final norm, so it must follow the only layer"
    assert seq % TM_PROJ == 0 and n_prompt % TM_TOK == 0 and seq % TQ == 0
    assert n_sample == MLP_CHUNK and MLP_CHUNK % dseq == 0 and n_sample <= TM_TOK
    n_pad = n_prompt + TM_TOK
    n_tiles_moe = (n_pad * TOP_K) // TM_MOE + N_EXPERTS
    n_slots = n_tiles_moe * TM_MOE

    xp = x_prompt.reshape(n_prompt, D_MODEL)
    xs = x_sample.reshape(n_sample, D_MODEL)
    outs = {k: [] for k in ("kp", "vp", "lfp", "ks", "vs", "lfs", "gvs")}
    for l in range(depth):
        win_b = jnp.pad(w_in[l], ((0, 0), (0, W_IN_COLS - w_in.shape[2]))).astype(BF16)
        bf_pad = jnp.pad(b_f[l], (0, LANES - N_B_HEADS)).reshape(1, LANES)
        nm = norm_mix[l].reshape(1, D_MODEL)
        nav = norm_av[l].reshape(1, D_A)
        reps = MLP_CHUNK // dseq
        ws_s = jnp.tile(w_s[l][:, :dseq, :dseq], (1, reps, reps))
        bst_p = b_s[l].T
        bst_s = jnp.tile(b_s[l][:, :dseq].T, (reps, 1))

        oa_p, q_p, k_p, v_p, kb_p, vt_p, lf_p, _ = _in_projection(
            xp, nm, win_b, bf_pad, nav, w_s[l], bst_p, tm=TM_PROJ, period=MLP_CHUNK, seq_for_vt=seq)
        lf_p3 = lf_p.reshape(bsz, seq, N_B_HEADS)
        _, *neg_c_pieces = _cumsum_time(lf_p3.transpose(0, 2, 1))
        ob_p = _attention_prompt(q_p.reshape(bsz, seq, D_B), kb_p.reshape(bsz, seq, D_B),
                                 vt_p, _bias_lanes(neg_c_pieces))

        oa_s, q_s, k_s, v_s, kb_s, vb_s, lf_s, vn_s = _in_projection(
            xs, nm, win_b, bf_pad, nav, ws_s, bst_s, tm=n_sample, period=dseq)
        lf_s3 = lf_s.reshape(dbsz, dseq, N_B_HEADS)
        t_all = past + dseq
        t_padded = -(-t_all // CS_BLK) * CS_BLK
        lf_all = jnp.concatenate([cache_logf[l].astype(F32), lf_s3], axis=1).transpose(0, 2, 1)
        c_all = _cumsum_time(jnp.pad(lf_all, ((0, 0), (0, 0), (0, t_padded - t_all))))[0]
        ob_s = _attention_sample(q_s.reshape(dbsz, dseq, D_B),
                                 cache_k[l], cache_v[l],
                                 kb_s.reshape(dbsz, dseq, D_B), vb_s.reshape(dbsz, dseq, D_B),
                                 _head_pairs(c_all[:, :, :past]), _head_pairs(c_all[:, :, past:t_all]))

        pad_rows = lambda a: jnp.pad(a, ((0, TM_TOK - n_sample), (0, 0)))
        wr_b = w_router[l].T.astype(BF16)
        br_pad = b_router[l].reshape(N_EXPERTS, 1)
        x2, h3, tidx, tw, rank, cnt = _tail(
            oa_p, ob_p.reshape(n_prompt, D_B), xp, pad_rows(oa_s), pad_rows(ob_s.reshape(n_sample, D_B)),
            pad_rows(xs), norm_out_a[l].reshape(1, D_A), norm_out_b[l].reshape(1, D_B), w_out[l].astype(BF16),
            norm_ffn[l].reshape(1, D_MODEL), wr_b, br_pad)
        counts = cnt[:, 0].astype(I32)
        slot, te, n_used, pad_start, pad_len = _routing_tables(tidx[:TOP_K], rank[:TOP_K], counts, n_tiles_moe)
        tw = tw[:TOP_K].T
        xsort = _dispatch(pad_start, pad_len, n_used, slot, h3, n_slots)
        ysort = _moe(te, n_used, xsort, w1[l], b1[l].reshape(N_EXPERTS, 1, 2 * D_FF), w2[l],
                     b2[l].reshape(N_EXPERTS, 1, D_MODEL))
        xp, y_tail = _combine(slot, ysort, x2, tw, norm_final.reshape(1, D_MODEL))
        xs = y_tail[:n_sample]

        outs["kp"].append(k_p.reshape(bsz, seq, N_B_HEADS, HEAD_DIM))
        outs["vp"].append(v_p.reshape(bsz, seq, N_B_HEADS, HEAD_DIM))
        outs["lfp"].append(lf_p3)
        outs["ks"].append(k_s.reshape(dbsz, dseq, N_B_HEADS, HEAD_DIM))
        outs["vs"].append(v_s.reshape(dbsz, dseq, N_B_HEADS, HEAD_DIM))
        outs["lfs"].append(lf_s3)
        outs["gvs"].append(vn_s.reshape(dbsz, dseq, D_A))

    y_prompt = xp.reshape(bsz, seq, D_MODEL)
    y_sample = xs.reshape(dbsz, dseq, D_MODEL)
    st = jnp.stack
    return (y_prompt, y_sample, st(outs["kp"]), st(outs["vp"]), st(outs["lfp"]),
            st(outs["ks"]), st(outs["vs"]), st(outs["lfs"]), st(outs["gvs"]))
```
